```python
import math
import jax
import jax.numpy as jnp
from jax import lax
import numpy as np

D_MODEL = 4096
BATCH = 1
SEQ = 16384
DEPTH = 4

GRID_W = 64
CTX_LEN = 256
N_MIXERS = 3
FFN_HIDDEN = ((8 * D_MODEL + 3 * 256 - 1) // (3 * 256)) * 256
ADA_RANK = 256
N_MOD = 6
EPS = 1e-6
MLA_HEADS = D_MODEL // 128
Q_RANK = D_MODEL // 4
KV_RANK = 512
NOPE_DIM = 128
ROPE_DIM = 64
V_DIM = 128
ROPE_BASE = 10000.0
Q_BLOCK = 128
HY_BANDS = 16
HY_EMB = 2 * HY_BANDS + 1
HY_FILTER_HIDDEN = 64
HY_DECAY_TARGET = 1e-2
HY_FAST_DECAY_PCT = 0.3
HY_SLOW_DECAY_PCT = 1.5

kernel_name = 'hybrid_interleaved_diffusion_trunk'


def rmsnorm(x, gain):
    xf = x.astype(jnp.float32)
    y = xf * lax.rsqrt(jnp.mean(xf * xf, axis=-1, keepdims=True) + EPS)
    return (y * gain.astype(jnp.float32)).astype(x.dtype)


def modulate(x, gain, shift, scale):
    return rmsnorm(x, gain) * (1 + scale) + shift


def adaln(s, w_down, w_up, bias):
    mod = (s @ w_down) @ w_up + bias
    return jnp.split(mod[:, None, :], N_MOD, axis=-1)


def conv3(x, w):
    xp = jnp.pad(x, ((0, 0), (1, 1), (0, 0)))
    return w[0] * xp[:, :-2] + w[1] * xp[:, 1:-1] + w[2] * xp[:, 2:]


def swiglu(h, w_gate_up, w_down):
    gate, up = jnp.split(h @ w_gate_up, 2, axis=-1)
    return (jax.nn.silu(gate) * up) @ w_down


def short_conv_mixer(h, w_in, w_conv, w_out):
    b_gate, c_gate, xv = jnp.split(h @ w_in, 3, axis=-1)
    return (b_gate * conv3(c_gate * xv, w_conv)) @ w_out


def grid_rope_tables(n):
    rows = n // GRID_W
    row = jnp.repeat(jnp.arange(rows, dtype=jnp.float32), GRID_W)
    col = jnp.tile(jnp.arange(GRID_W, dtype=jnp.float32), rows)
    axis_dim = ROPE_DIM // 2
    inv = ROPE_BASE ** (-jnp.arange(0, axis_dim, 2, dtype=jnp.float32) / axis_dim)
    ang_r = row[:, None] * inv
    ang_c = col[:, None] * inv
    return (jnp.cos(ang_r), jnp.sin(ang_r), jnp.cos(ang_c), jnp.sin(ang_c))


def rotate(x, cos, sin):
    x1, x2 = jnp.split(x, 2, axis=-1)
    return jnp.concatenate([x1 * cos - x2 * sin, x2 * cos + x1 * sin], axis=-1)


def axial_rope(x, tabs):
    cos_r, sin_r, cos_c, sin_c = tabs
    x_row, x_col = jnp.split(x.astype(jnp.float32), 2, axis=-1)
    return jnp.concatenate([rotate(x_row, cos_r, sin_r), rotate(x_col, cos_c, sin_c)], axis=-1).astype(x.dtype)


def mla_queries(cq, q_norm, w_uq):
    B, n, _ = cq.shape
    q = (rmsnorm(cq, q_norm) @ w_uq).reshape(B, n, MLA_HEADS, NOPE_DIM + ROPE_DIM)
    return q[..., :NOPE_DIM], q[..., NOPE_DIM:]


def mla_keys_values(ckv_kr, kv_norm, w_ukv):
    B, n, _ = ckv_kr.shape
    ckv, k_rope = ckv_kr[..., :KV_RANK], ckv_kr[..., KV_RANK:]
    kv = (rmsnorm(ckv, kv_norm) @ w_ukv).reshape(B, n, MLA_HEADS, NOPE_DIM + V_DIM)
    return kv[..., :NOPE_DIM], k_rope, kv[..., NOPE_DIM:]


def attend(q_nope, q_rope, k_nope, k_rope, v):
    scale = (NOPE_DIM + ROPE_DIM) ** -0.5
    s = jnp.einsum('bqhd,bkhd->bhqk', q_nope, k_nope) + jnp.einsum('bqhr,bkr->bhqk', q_rope, k_rope)
    p = jax.nn.softmax(s.astype(jnp.float32) * scale, axis=-1).astype(v.dtype)
    return jnp.einsum('bhqk,bkhd->bqhd', p, v)


def mla_mixer(h, hc, ctx_queries, rope_tabs, w_down, q_norm, kv_norm, w_uq, w_ukv, w_out):
    B, n, _ = h.shape
    d = h @ w_down
    q_nope, q_rope = mla_queries(d[..., :Q_RANK], q_norm, w_uq)
    k_nope, k_rope, v = mla_keys_values(d[..., Q_RANK:], kv_norm, w_ukv)
    q_rope = axial_rope(q_rope, tuple(t[:, None, :] for t in rope_tabs))
    k_rope = axial_rope(k_rope, rope_tabs)
    if ctx_queries:
        dc = hc @ w_down
        qn_c, qr_c = mla_queries(dc[..., :Q_RANK], q_norm, w_uq)
        kn_c, kr_c, v_c = mla_keys_values(dc[..., Q_RANK:], kv_norm, w_ukv)
    else:
        kn_c, kr_c, v_c = mla_keys_values(hc @ w_down[:, Q_RANK:], kv_norm, w_ukv)
    kn_all = jnp.concatenate([k_nope, kn_c], axis=1)
    kr_all = jnp.concatenate([k_rope, kr_c], axis=1)
    v_all = jnp.concatenate([v, v_c], axis=1)
    nb = n // Q_BLOCK
    to_blocks = lambda a: a.reshape(B, nb, Q_BLOCK, *a.shape[2:]).swapaxes(0, 1)
    o = lax.map(lambda qs: attend(qs[0], qs[1], kn_all, kr_all, v_all), (to_blocks(q_nope), to_blocks(q_rope)))
    o = o.swapaxes(0, 1).reshape(B, n, MLA_HEADS * V_DIM)
    y = o @ w_out
    yc = None
    if ctx_queries:
        oc = attend(qn_c, qr_c, kn_c, kr_c, v_c)
        yc = oc.reshape(B, oc.shape[1], MLA_HEADS * V_DIM) @ w_out
    return y, yc


def hyena_filter_spectrum(n, w1, b1, w2, b2, w3, b3, freq, w4):
    f32 = jnp.float32
    t = jnp.linspace(0.0, 1.0, n, dtype=f32)[:, None]
    w = (2.0 * math.pi / n) * jnp.arange(n, dtype=f32)[:, None]
    bands = jnp.linspace(1e-4, HY_BANDS - 1, HY_BANDS, dtype=f32)
    z = jnp.concatenate([t, jnp.cos(bands * w), -jnp.sin(bands * w)], axis=-1)
    fr = freq.astype(f32)
    hdn = jnp.sin(fr[0] * (z @ w1.astype(f32) + b1.astype(f32)))
    hdn = jnp.sin(fr[1] * (hdn @ w2.astype(f32) + b2.astype(f32)))
    hdn = jnp.sin(fr[2] * (hdn @ w3.astype(f32) + b3.astype(f32)))
    filt = hdn @ w4.astype(f32)
    d = filt.shape[-1] // 2
    deltas = jnp.abs(jnp.linspace(math.log(HY_DECAY_TARGET) / HY_SLOW_DECAY_PCT,
                                  math.log(HY_DECAY_TARGET) / HY_FAST_DECAY_PCT, d, dtype=f32))
    decay = jnp.exp(-t * deltas)
    h_fwd = filt[:, :d] * decay
    h_bwd = filt[:, d:] * decay
    g = jnp.concatenate([h_fwd, jnp.zeros((1, d), f32), h_bwd[:0:-1]], axis=0)
    return jnp.fft.rfft(g, axis=0)


def hyena_mixer(h, w_in, conv_w, conv_b, f_w1, f_b1, f_w2, f_b2, f_w3, f_b3, f_freq, f_w4, skip, w_out):
    n = h.shape[1]
    z = conv3(h @ w_in, conv_w) + conv_b
    x0, x1, v = jnp.split(z, 3, axis=-1)
    v = x1 * v
    spec = hyena_filter_spectrum(n, f_w1, f_b1, f_w2, f_b2, f_w3, f_b3, f_freq, f_w4)
    vf = jnp.fft.rfft(v.astype(jnp.float32), n=2 * n, axis=1)
    y = jnp.fft.irfft(vf * spec, n=2 * n, axis=1)[:, :n].astype(h.dtype) + v * skip
    return (x0 * y) @ w_out


def setup_inputs(seed: int = 0) -> dict:
    key = jax.random.key(seed)
    keys = iter(jax.random.split(key, 64))
    f32 = jnp.float32
    D = D_MODEL
    FH = HY_FILTER_HIDDEN

    def dense(shape, fan_in, scale=1.0):
        return jax.random.normal(next(keys), shape, f32) * (scale * fan_in ** -0.5)

    def near_one(shape):
        return 1.0 + 0.05 * jax.random.normal(next(keys), shape, f32)

    def small(shape, s=0.01):
        return s * jax.random.normal(next(keys), shape, f32)

    n_a, n_b, n_c = (len(range(m, DEPTH, N_MIXERS)) for m in range(N_MIXERS))
    return {
        'x': jax.random.normal(next(keys), (BATCH, SEQ, D), f32),
        'c': jax.random.normal(next(keys), (BATCH, D), f32),
        'ctx': jax.random.normal(next(keys), (BATCH, CTX_LEN, D), f32),
        'c_ctx': jax.random.normal(next(keys), (D,), f32),
        'ada_down': dense((DEPTH, D, ADA_RANK), D),
        'ada_up': dense((DEPTH, ADA_RANK, N_MOD * D), ADA_RANK, 0.5),
        'ada_bias': small((DEPTH, N_MOD * D)),
        'norm_gain': near_one((DEPTH, 4, D)),
        'ffn_w_gate_up': dense((DEPTH, D, 2 * FFN_HIDDEN), D),
        'ffn_w_down': dense((DEPTH, FFN_HIDDEN, D), FFN_HIDDEN),
        'sc_w_in': dense((n_a, D, 3 * D), D),
        'sc_conv': dense((n_a, 3, D), 3),
        'sc_w_out': dense((n_a, D, D), D),
        'mla_w_down': dense((n_b, D, Q_RANK + KV_RANK + ROPE_DIM), D),
        'mla_q_norm': near_one((n_b, Q_RANK)),
        'mla_kv_norm': near_one((n_b, KV_RANK)),
        'mla_w_uq': dense((n_b, Q_RANK, MLA_HEADS * (NOPE_DIM + ROPE_DIM)), Q_RANK),
        'mla_w_ukv': dense((n_b, KV_RANK, MLA_HEADS * (NOPE_DIM + V_DIM)), KV_RANK),
        'mla_w_out': dense((n_b, MLA_HEADS * V_DIM, D), MLA_HEADS * V_DIM),
        'hy_w_in': dense((n_c, D, 3 * D), D),
        'hy_conv': dense((n_c, 3, 3 * D), 3),
        'hy_conv_b': small((n_c, 3 * D)),
        'hy_f_w1': dense((n_c, HY_EMB, FH), HY_EMB),
        'hy_f_b1': small((n_c, FH), 0.1),
        'hy_f_w2': dense((n_c, FH, FH), FH),
        'hy_f_b2': small((n_c, FH), 0.1),
        'hy_f_w3': dense((n_c, FH, FH), FH),
        'hy_f_b3': small((n_c, FH), 0.1),
        'hy_f_freq': near_one((n_c, 3, FH)),
        'hy_f_w4': dense((n_c, FH, 2 * D), FH),
        'hy_skip': jax.random.normal(next(keys), (n_c, D), f32),
        'hy_w_out': dense((n_c, D, D), D),
    }


def reference(x, c, ctx, c_ctx, ada_down, ada_up, ada_bias, norm_gain, ffn_w_gate_up, ffn_w_down,
              sc_w_in, sc_conv, sc_w_out,
              mla_w_down, mla_q_norm, mla_kv_norm, mla_w_uq, mla_w_ukv, mla_w_out,
              hy_w_in, hy_conv, hy_conv_b, hy_f_w1, hy_f_b1, hy_f_w2, hy_f_b2, hy_f_w3, hy_f_b3,
              hy_f_freq, hy_f_w4, hy_skip, hy_w_out):
    n = x.shape[1]
    rope_tabs = grid_rope_tables(n)
    mla_layers = [i for i in range(DEPTH) if i % N_MIXERS == 1]
    last_ctx_read = mla_layers[-1] if mla_layers else -1
    s_lat = jax.nn.silu(c)
    s_ctx = jax.nn.silu(c_ctx)[None, :]
    for i in range(DEPTH):
        kind, j = i % N_MIXERS, i // N_MIXERS
        ctx_full = i < last_ctx_read
        ctx_keys = i == last_ctx_read
        g = norm_gain[i]
        mod = adaln(s_lat, ada_down[i], ada_up[i], ada_bias[i])
        h = modulate(x, g[0], mod[0], mod[1])
        if ctx_full or ctx_keys:
            mod_c = adaln(s_ctx, ada_down[i], ada_up[i], ada_bias[i])
            hc = modulate(ctx, g[0], mod_c[0], mod_c[1])
        yc = None
        if kind == 0:
            y = short_conv_mixer(h, sc_w_in[j], sc_conv[j], sc_w_out[j])
            if ctx_full:
                yc = short_conv_mixer(hc, sc_w_in[j], sc_conv[j], sc_w_out[j])
        elif kind == 1:
            y, yc = mla_mixer(h, hc, ctx_full, rope_tabs, mla_w_down[j], mla_q_norm[j], mla_kv_norm[j],
                              mla_w_uq[j], mla_w_ukv[j], mla_w_out[j])
        else:
            hp = (hy_w_in[j], hy_conv[j], hy_conv_b[j], hy_f_w1[j], hy_f_b1[j], hy_f_w2[j], hy_f_b2[j],
                  hy_f_w3[j], hy_f_b3[j], hy_f_freq[j], hy_f_w4[j], hy_skip[j], hy_w_out[j])
            y = hyena_mixer(h, *hp)
            if ctx_full:
                yc = hyena_mixer(hc, *hp)
        x = x + mod[2] * rmsnorm(y, g[1])
        x = x + mod[5] * rmsnorm(swiglu(modulate(x, g[2], mod[3], mod[4]), ffn_w_gate_up[i], ffn_w_down[i]), g[3])
        if ctx_full:
            ctx = ctx + mod_c[2] * rmsnorm(yc, g[1])
            ctx = ctx + mod_c[5] * rmsnorm(swiglu(modulate(ctx, g[2], mod_c[3], mod_c[4]), ffn_w_gate_up[i], ffn_w_down[i]), g[3])
    return x
```

```python
import functools
import math

import jax
import jax.numpy as jnp
from jax import lax
from jax.experimental import pallas as pl
from jax.experimental.pallas import tpu as pltpu

F32 = jnp.float32
BF16 = jnp.bfloat16

EPS = 1e-6
N_MOD = 6
NOPE_DIM = 128
ROPE_DIM = 64
V_DIM = 128
GRID_W = 64
ROPE_BASE = 10000.0
HY_BANDS = 16
HY_DECAY_TARGET = 1e-2
HY_FAST_DECAY_PCT = 0.3
HY_SLOW_DECAY_PCT = 1.5

LANES = 128
SUBLANES = 8
VMEM_LIMIT_BYTES = 56 * 1024 * 1024
DFT_INNER = 128


def _tile(dim, pref, align):
    best = None
    t = align
    while t <= min(dim, pref):
        if dim % t == 0:
            best = t
        t += align
    return best if best is not None else dim


def _params(sem):
    return pltpu.CompilerParams(dimension_semantics=sem, vmem_limit_bytes=VMEM_LIMIT_BYTES)


def _split_hi_lo(x):
    hi = x.astype(BF16)
    lo = (x - hi.astype(F32)).astype(BF16)
    return hi, lo


def _dot(a, b):
    return jnp.dot(a, b, preferred_element_type=F32)


def _dot3(a, b):
    ah, al = _split_hi_lo(a)
    bh, bl = _split_hi_lo(b)
    return _dot(ah, bh) + _dot(ah, bl) + _dot(al, bh)


def _dot3_pre(ah, al, b):
    bh, bl = _split_hi_lo(b)
    return _dot(ah, bh) + _dot(ah, bl) + _dot(al, bh)


def _rms(x, gain):
    return x * lax.rsqrt(jnp.mean(x * x, axis=-1, keepdims=True) + EPS) * gain


def _silu(x):
    return x * (1.0 / (1.0 + jnp.exp(-x)))


def _mm_body(*refs, n_w, n_extra, n_out, nk, epilogue):
    a_ref = refs[0]
    w_refs = refs[1:1 + n_w]
    extra_refs = refs[1 + n_w:1 + n_w + n_extra]
    out_refs = refs[1 + n_w + n_extra:1 + n_w + n_extra + n_out]
    acc_refs = refs[1 + n_w + n_extra + n_out:]
    a = a_ref[...].astype(BF16)
    dots = [_dot(a, w[...].astype(BF16)) for w in w_refs]
    if nk == 1:
        epilogue(dots, extra_refs, out_refs)
        return
    k = pl.program_id(2)

    @pl.when(k == 0)
    def _():
        for acc, d in zip(acc_refs, dots):
            acc[...] = d

    @pl.when(k > 0)
    def _():
        for acc, d in zip(acc_refs, dots):
            acc[...] += d

    @pl.when(k == nk - 1)
    def _():
        epilogue([acc[...] for acc in acc_refs], extra_refs, out_refs)


def _matmul(a, ws, outs, epilogue, *, extras=(), tm, tn, tk=None, name):
    M, K = a.shape
    tk = K if tk is None else tk
    nk = K // tk
    n_col_blocks = outs[0][0] // outs[0][1]
    grid = (M // tm, n_col_blocks, nk)
    in_specs = [pl.BlockSpec((tm, tk), lambda i, j, k: (i, k))]
    operands = [a]
    for w, off in ws:
        in_specs.append(pl.BlockSpec((tk, tn), functools.partial(lambda i, j, k, o: (k, j + o), o=off)))
        operands.append(w)
    for arr, bshape, imap in extras:
        in_specs.append(pl.BlockSpec(bshape, imap))
        operands.append(arr)
    out_shape = [jax.ShapeDtypeStruct((M, wt), dt) for wt, _, dt in outs]
    out_specs = [pl.BlockSpec((tm, bw), lambda i, j, k: (i, j)) for _, bw, _ in outs]
    scratch = [pltpu.VMEM((tm, tn), F32) for _ in ws] if nk > 1 else []
    body = functools.partial(_mm_body, n_w=len(ws), n_extra=len(extras), n_out=len(outs), nk=nk,
                             epilogue=epilogue)
    res = pl.pallas_call(
        body, grid=grid, in_specs=in_specs, out_specs=out_specs, out_shape=out_shape,
        scratch_shapes=scratch, name=name,
        compiler_params=_params(("parallel", "arbitrary", "arbitrary")),
    )(*operands)
    return res


def _ep_plain(dots, extras, outs):
    outs[0][...] = dots[0].astype(outs[0].dtype)


def _ep_gate_pair(dots, extras, outs):
    outs[0][...] = dots[0].astype(outs[0].dtype)
    outs[1][...] = (dots[1] * dots[2]).astype(outs[1].dtype)


def _ep_swiglu(dots, extras, outs):
    outs[0][...] = (_silu(dots[0]) * dots[1]).astype(outs[0].dtype)


def _ep_rms(dots, extras, outs):
    outs[0][...] = _rms(dots[0], extras[0][...]).astype(outs[0].dtype)


def _ep_kv_down(dots, extras, outs, *, kv_rank):
    gain_ref, tab_ref = extras
    d = dots[0]
    outs[0][...] = _rms(d[:, :kv_rank], gain_ref[...]).astype(outs[0].dtype)
    p = d[:, kv_rank:] * tab_ref[...]
    outs[1][...] = (p + pltpu.roll(p, ROPE_DIM, axis=1)).astype(outs[1].dtype)


def _ep_q_up(dots, extras, outs):
    tab = extras[0][...]
    x = dots[0]
    width = tab.shape[1]
    for g in range(x.shape[1] // width):
        outs[0][:, g * width:(g + 1) * width] = (x[:, g * width:(g + 1) * width] * tab).astype(outs[0].dtype)


def _ep_kv_up(dots, extras, outs):
    kn, v = dots
    kr = extras[0][...]
    for g in range(kn.shape[1] // NOPE_DIM):
        base = g * (NOPE_DIM + LANES)
        outs[0][:, base:base + NOPE_DIM] = kn[:, g * NOPE_DIM:(g + 1) * NOPE_DIM].astype(outs[0].dtype)
        outs[0][:, base + NOPE_DIM:base + NOPE_DIM + LANES] = kr
    outs[1][...] = v.astype(outs[1].dtype)


def _adaln_body(s_ref, down_ref, up_ref, bias_ref, out_ref, t_ref):
    @pl.when(pl.program_id(1) == 0)
    def _():
        t_ref[...] = _dot3(_silu(s_ref[...]), down_ref[0])

    out_ref[0] = _dot3(t_ref[...], up_ref[0]) + bias_ref[0]


def _adaln(s_raw, ada_down, ada_up, ada_bias):
    depth, d, r = ada_down.shape
    n6 = ada_up.shape[2]
    rows = s_raw.shape[0]
    tn = _tile(n6, 2048, LANES)
    return pl.pallas_call(
        _adaln_body, grid=(depth, n6 // tn),
        in_specs=[pl.BlockSpec((rows, d), lambda i, j: (0, 0)),
                  pl.BlockSpec((1, d, r), lambda i, j: (i, 0, 0)),
                  pl.BlockSpec((1, r, tn), lambda i, j: (i, 0, j)),
                  pl.BlockSpec((1, 1, tn), lambda i, j: (i, 0, j))],
        out_specs=pl.BlockSpec((1, rows, tn), lambda i, j: (i, 0, j)),
        out_shape=jax.ShapeDtypeStruct((depth, rows, n6), F32),
        scratch_shapes=[pltpu.VMEM((rows, r), F32)], name="adaln",
        compiler_params=_params(("arbitrary", "arbitrary")),
    )(s_raw, ada_down, ada_up, ada_bias.reshape(depth, 1, n6))


def _rnm_body(*refs, has_resid, has_h):
    refs = list(refs)
    x_ref = refs.pop(0)
    y_ref = refs.pop(0) if has_resid else None
    vec_ref = refs.pop(0)
    x = x_ref[...]
    if has_resid:
        xo_ref = refs.pop(0)
        x = x + vec_ref[0:1, :] * _rms(y_ref[...].astype(F32), vec_ref[1:2, :])
        xo_ref[...] = x
    if has_h:
        h_ref = refs.pop(0)
        h_ref[...] = (_rms(x, vec_ref[2:3, :]) * (1.0 + vec_ref[4:5, :]) + vec_ref[3:4, :]).astype(h_ref.dtype)


def _resid_norm_mod(x, y, vec, *, has_h, rows=None):
    n, d = x.shape
    n = n if rows is None else rows
    has_resid = y is not None
    tr = _tile(n, 256, 16)
    spec = pl.BlockSpec((tr, d), lambda i: (i, 0))
    in_specs = [spec] + ([spec] if has_resid else []) + [pl.BlockSpec(vec.shape, lambda i: (0, 0))]
    operands = [x] + ([y] if has_resid else []) + [vec]
    out_shape, out_specs = [], []
    if has_resid:
        out_shape.append(jax.ShapeDtypeStruct((n, d), F32))
        out_specs.append(spec)
    if has_h:
        out_shape.append(jax.ShapeDtypeStruct((n, d), BF16))
        out_specs.append(spec)
    res = pl.pallas_call(
        functools.partial(_rnm_body, has_resid=has_resid, has_h=has_h), grid=(n // tr,),
        in_specs=in_specs, out_specs=out_specs, out_shape=out_shape, name="resid_norm_mod",
        compiler_params=_params(("parallel",)),
    )(*operands)
    res = list(res)
    x_new = res.pop(0) if has_resid else None
    h = res.pop(0) if has_h else None
    return x_new, h


def _vec_rows(d, *rows):
    out = [r.reshape(1, d).astype(F32) for r in rows]
    out += [jnp.zeros((1, d), F32)] * (SUBLANES - len(out))
    return jnp.concatenate(out, axis=0)


def _conv3_rows(main, prev_blk, next_blk, w0, w1, w2, is_first, is_last):
    tr = main.shape[0]
    rows = lax.broadcasted_iota(jnp.int32, main.shape, 0)
    prev_row = jnp.where(is_first, 0.0, prev_blk[SUBLANES - 1:SUBLANES, :])
    next_row = jnp.where(is_last, 0.0, next_blk[0:1, :])
    up = jnp.where(rows == 0, prev_row, pltpu.roll(main, 1, axis=0))
    dn = jnp.where(rows == tr - 1, next_row, pltpu.roll(main, tr - 1, axis=0))
    return w0 * up + w1 * main + w2 * dn


def _halo_specs(tr, tc, n_rows, col_off):
    per = tr // SUBLANES
    last = n_rows // SUBLANES - 1
    return [
        pl.BlockSpec((tr, tc), lambda i, j: (i, j + col_off)),
        pl.BlockSpec((SUBLANES, tc), lambda i, j: (jnp.maximum(i * per - 1, 0), j + col_off)),
        pl.BlockSpec((SUBLANES, tc), lambda i, j: (jnp.minimum((i + 1) * per, last), j + col_off)),
    ]


def _sc_gate_body(b_ref, u_ref, up_ref, un_ref, w_ref, o_ref):
    i = pl.program_id(0)
    conv = _conv3_rows(u_ref[...], up_ref[...], un_ref[...], w_ref[0:1, :], w_ref[1:2, :], w_ref[2:3, :],
                       i == 0, i == pl.num_programs(0) - 1)
    o_ref[...] = (b_ref[...] * conv).astype(o_ref.dtype)


def _sc_gate(b, u, w_conv):
    n, d = u.shape
    tr = _tile(n, 512, 16)
    tc = _tile(d, 1024, LANES)
    w8 = jnp.concatenate([w_conv.astype(F32), jnp.zeros((SUBLANES - 3, d), F32)], axis=0)
    return pl.pallas_call(
        _sc_gate_body, grid=(n // tr, d // tc),
        in_specs=[pl.BlockSpec((tr, tc), lambda i, j: (i, j))] + _halo_specs(tr, tc, n, 0)
        + [pl.BlockSpec((SUBLANES, tc), lambda i, j: (0, j))],
        out_specs=pl.BlockSpec((tr, tc), lambda i, j: (i, j)),
        out_shape=jax.ShapeDtypeStruct((n, d), BF16), name="sc_conv_gate",
        compiler_params=_params(("parallel", "parallel")),
    )(b, u, u, u, w8)


def _hy_gate_body(*refs):
    z_refs = refs[0:9]
    w_refs = refs[9:12]
    x0_ref, vv_ref = refs[12:14]
    i = pl.program_id(0)
    first, last = i == 0, i == pl.num_programs(0) - 1
    conv = []
    for g in range(3):
        m, p, nx = z_refs[3 * g:3 * g + 3]
        w = w_refs[g]
        conv.append(_conv3_rows(m[...], p[...], nx[...], w[0:1, :], w[1:2, :], w[2:3, :], first, last) + w[3:4, :])
    x0_ref[...] = conv[0]
    vv_ref[...] = conv[1] * conv[2]


def _hy_gate(z, conv_w, conv_b):
    n, d3 = z.shape
    d = d3 // 3
    tr = _tile(n, 512, 16)
    tc = _tile(d, 512, LANES)
    ncb = d // tc
    w8 = jnp.concatenate([conv_w.astype(F32), conv_b.reshape(1, d3).astype(F32),
                          jnp.zeros((SUBLANES - 4, d3), F32)], axis=0)
    in_specs, operands = [], []
    for g in range(3):
        in_specs += _halo_specs(tr, tc, n, g * ncb)
        operands += [z, z, z]
    for g in range(3):
        in_specs.append(pl.BlockSpec((SUBLANES, tc), functools.partial(lambda i, j, o: (0, j + o), o=g * ncb)))
        operands.append(w8)
    spec = pl.BlockSpec((tr, tc), lambda i, j: (i, j))
    return pl.pallas_call(
        _hy_gate_body, grid=(n // tr, ncb), in_specs=in_specs, out_specs=[spec, spec],
        out_shape=[jax.ShapeDtypeStruct((n, d), F32)] * 2, name="hy_conv_gate",
        compiler_params=_params(("parallel", "parallel")),
    )(*operands)


def _attn_body(q_ref, k_ref, v_ref, o_ref, m_ref, l_ref, acc_ref):
    ki = pl.program_id(2)

    @pl.when(ki == 0)
    def _():
        m_ref[...] = jnp.full(m_ref.shape, -jnp.inf, F32)
        l_ref[...] = jnp.zeros(l_ref.shape, F32)
        acc_ref[...] = jnp.zeros(acc_ref.shape, F32)

    s = lax.dot_general(q_ref[...], k_ref[...], (((1,), (1,)), ((), ())), preferred_element_type=F32)
    m_prev = m_ref[...]
    m_new = jnp.maximum(m_prev, jnp.max(s, axis=1, keepdims=True))
    alpha = jnp.exp(m_prev - m_new)
    p = jnp.exp(s - m_new[:, 0:1])
    l_ref[...] = alpha * l_ref[...] + jnp.sum(p, axis=1, keepdims=True)
    acc_ref[...] = alpha * acc_ref[...] + _dot(p.astype(BF16), v_ref[...])
    m_ref[...] = m_new

    @pl.when(ki == pl.num_programs(2) - 1)
    def _():
        o_ref[...] = (acc_ref[...] / l_ref[...]).astype(o_ref.dtype)


def _attention(q, k, v, heads):
    n = q.shape[0]
    nk = k.shape[0]
    qw = q.shape[1] // heads
    tq = _tile(n, 1024, 16)
    tk = _tile(nk, 1664, LANES)
    return pl.pallas_call(
        _attn_body, grid=(heads, n // tq, nk // tk),
        in_specs=[pl.BlockSpec((tq, qw), lambda h, i, j: (i, h)),
                  pl.BlockSpec((tk, qw), lambda h, i, j: (j, h)),
                  pl.BlockSpec((tk, V_DIM), lambda h, i, j: (j, h))],
        out_specs=pl.BlockSpec((tq, V_DIM), lambda h, i, j: (i, h)),
        out_shape=jax.ShapeDtypeStruct((n, heads * V_DIM), BF16),
        scratch_shapes=[pltpu.VMEM((tq, V_DIM), F32)] * 3, name="mla_flash_attention",
        compiler_params=_params(("parallel", "parallel", "arbitrary")),
    )(q, k, v)


def _filter_body(z_ref, w1_ref, w2_ref, w3_ref, bf_ref, w4_ref, dl_ref, o_ref, h_ref):
    @pl.when(pl.program_id(1) == 0)
    def _():
        bf = bf_ref[...]
        h = jnp.sin(bf[3:4, :] * (_dot3(z_ref[...], w1_ref[...]) + bf[0:1, :]))
        h = jnp.sin(bf[4:5, :] * (_dot3(h, w2_ref[...]) + bf[1:2, :]))
        h_ref[...] = jnp.sin(bf[5:6, :] * (_dot3(h, w3_ref[...]) + bf[2:3, :]))

    z = z_ref[...]
    t = z[:, 0:1]
    sign = z[:, HY_BANDS * 2 + 1:HY_BANDS * 2 + 2]
    filt = _dot3(h_ref[...], w4_ref[...])
    o_ref[...] = sign * filt * jnp.exp(-t * dl_ref[...])


def _hyena_filter(n, d, f_w1, f_b1, f_w2, f_b2, f_w3, f_b3, f_freq, f_w4):
    fh = f_w1.shape[1]
    emb = f_w1.shape[0]
    r = jnp.arange(2 * n, dtype=jnp.int32)
    p = jnp.minimum(jnp.where(r < n, r, 2 * n - r), n - 1)
    t = jnp.linspace(0.0, 1.0, n, dtype=F32)[p][:, None]
    w = ((2.0 * math.pi / n) * jnp.arange(n, dtype=F32))[p][:, None]
    bands = jnp.linspace(1e-4, HY_BANDS - 1, HY_BANDS, dtype=F32)
    sign = jnp.where(r < n, 1.0, jnp.where(r == n, 0.0, -1.0)).astype(F32)[:, None]
    zw = LANES // 2
    z = jnp.concatenate([t, jnp.cos(bands * w), -jnp.sin(bands * w), sign,
                         jnp.zeros((2 * n, zw - emb - 1), F32)], axis=-1)
    w1p = jnp.concatenate([f_w1.astype(F32), jnp.zeros((zw - emb, fh), F32)], axis=0)
    bf = jnp.concatenate([f_b1.reshape(1, fh), f_b2.reshape(1, fh), f_b3.reshape(1, fh),
                          f_freq.reshape(3, fh), jnp.zeros((2, fh), F32)], axis=0).astype(F32)
    deltas = jnp.abs(jnp.linspace(math.log(HY_DECAY_TARGET) / HY_SLOW_DECAY_PCT,
                                  math.log(HY_DECAY_TARGET) / HY_FAST_DECAY_PCT, d, dtype=F32)).reshape(1, d)
    tr = _tile(n, 512, SUBLANES)
    tc = _tile(d, 1024, LANES)
    ncb = d // tc
    half = n // tr
    const = lambda shape: pl.BlockSpec(shape, lambda i, j: (0, 0))
    return pl.pallas_call(
        _filter_body, grid=(2 * n // tr, ncb),
        in_specs=[pl.BlockSpec((tr, zw), lambda i, j: (i, 0)), const((zw, fh)), const((fh, fh)), const((fh, fh)),
                  const((SUBLANES, fh)),
                  pl.BlockSpec((fh, tc), lambda i, j: (0, j + jnp.where(i >= half, ncb, 0))),
                  pl.BlockSpec((1, tc), lambda i, j: (0, j))],
        out_specs=pl.BlockSpec((tr, tc), lambda i, j: (i, j)),
        out_shape=jax.ShapeDtypeStruct((2 * n, d), F32),
        scratch_shapes=[pltpu.VMEM((tr, fh), F32)], name="hyena_filter",
        compiler_params=_params(("parallel", "arbitrary")),
    )(z, w1p, f_w2.astype(F32), f_w3.astype(F32), bf, f_w4.astype(F32), deltas)


def _dft_tables(n):
    big_n = 2 * n
    n2 = DFT_INNER
    n1 = big_n // n2
    hk = n1 // 2
    k1 = jnp.arange(hk, dtype=jnp.int32)
    m1 = jnp.arange(n1, dtype=jnp.int32)
    ang_a = (math.pi / n1) * ((m1[None, :] * (2 * k1[:, None] + 1)) % (2 * n1)).astype(F32)
    fwd = jnp.concatenate([jnp.cos(ang_a), -jnp.sin(ang_a)], axis=0)
    inv = (2.0 / big_n) * jnp.concatenate([jnp.cos(ang_a[:, :hk]).T, -jnp.sin(ang_a[:, :hk]).T], axis=1)
    k2 = jnp.arange(n2, dtype=jnp.int32)
    m2 = jnp.arange(n2, dtype=jnp.int32)
    freq = 2 * k1[:, None, None] + 1 + 2 * n1 * k2[None, :, None]
    ang_c = (math.pi / big_n) * ((m2[None, None, :] * freq) % (2 * big_n)).astype(F32)
    gr, gi = jnp.cos(ang_c), -jnp.sin(ang_c)
    mid = jnp.concatenate([jnp.concatenate([gr, -gi], axis=2), jnp.concatenate([gi, gr], axis=2)], axis=1)
    mid_t = jnp.swapaxes(mid, 1, 2)
    return fwd, inv, mid, mid_t


def _dft_rows_body(fh_ref, fl_ref, x_ref, o_ref):
    o_ref[...] = _dot3_pre(fh_ref[...], fl_ref[...], x_ref[...])


def _dft_rows(f, x2d, tn):
    r, k = f.shape
    c = x2d.shape[1]
    fh, fl = _split_hi_lo(f)
    return pl.pallas_call(
        _dft_rows_body, grid=(c // tn,),
        in_specs=[pl.BlockSpec((r, k), lambda j: (0, 0)), pl.BlockSpec((r, k), lambda j: (0, 0)),
                  pl.BlockSpec((k, tn), lambda j: (0, j))],
        out_specs=pl.BlockSpec((r, tn), lambda j: (0, j)),
        out_shape=jax.ShapeDtypeStruct((r, c), F32), name="hyena_dft_outer",
        compiler_params=_params(("parallel",)),
    )(fh, fl, x2d)


def _dft_rows_inv_body(fh_ref, fl_ref, b_ref, v_ref, x0_ref, skip_ref, o_ref):
    y = _dot3_pre(fh_ref[...], fl_ref[...], b_ref[...])
    v = v_ref[...]
    o_ref[...] = (x0_ref[...] * (y + v * skip_ref[...])).astype(o_ref.dtype)


def _dft_rows_inv(f, b2d, v2d, x02d, skip, d):
    r, k = f.shape
    c = b2d.shape[1]
    fh, fl = _split_hi_lo(f)
    const = pl.BlockSpec((r, k), lambda j: (0, 0))
    col = pl.BlockSpec((r, d), lambda j: (0, j))
    return pl.pallas_call(
        _dft_rows_inv_body, grid=(c // d,),
        in_specs=[const, const, pl.BlockSpec((k, d), lambda j: (0, j)), col, col,
                  pl.BlockSpec((1, d), lambda j: (0, 0))],
        out_specs=col, out_shape=jax.ShapeDtypeStruct((r, c), BF16), name="hyena_dft_outer_inverse",
        compiler_params=_params(("parallel",)),
    )(fh, fl, b2d, v2d, x02d, skip.reshape(1, d).astype(F32))


def _mid_apply(mh_ref, ml_ref, re, im):
    n2 = re.shape[0]
    mh, ml = mh_ref[0], ml_ref[0]
    return (_dot3_pre(mh[:, :n2], ml[:, :n2], re) + _dot3_pre(mh[:, n2:], ml[:, n2:], im))


def _dft_mid_body(mh_ref, ml_ref, a_ref, o_ref):
    n2 = a_ref.shape[2]
    s = _mid_apply(mh_ref, ml_ref, a_ref[0, 0], a_ref[1, 0])
    o_ref[0, 0] = s[:n2]
    o_ref[1, 0] = s[n2:]


def _dft_conv_body(mh_ref, ml_ref, th_ref, tl_ref, a_ref, g_ref, o_ref):
    n2 = a_ref.shape[2]
    s = _mid_apply(mh_ref, ml_ref, a_ref[0, 0], a_ref[1, 0])
    sr, si = s[:n2], s[n2:]
    gr, gi = g_ref[0, 0], g_ref[1, 0]
    b = _mid_apply(th_ref, tl_ref, sr * gr - si * gi, sr * gi + si * gr)
    o_ref[0, 0] = b[:n2]
    o_ref[1, 0] = b[n2:]


def _dft_mid(mid, a4, spec4=None, mid_t=None):
    _, hk, n2, d = a4.shape
    tc = _tile(d, 1024, LANES)
    mspec = pl.BlockSpec((1, 2 * n2, 2 * n2), lambda k, j: (k, 0, 0))
    dspec = pl.BlockSpec((2, 1, n2, tc), lambda k, j: (0, k, 0, j))
    mh, ml = _split_hi_lo(mid)
    if spec4 is None:
        body, in_specs, operands, name = _dft_mid_body, [mspec, mspec, dspec], [mh, ml, a4], "hyena_dft_inner"
    else:
        th, tl = _split_hi_lo(mid_t)
        body, in_specs, operands = _dft_conv_body, [mspec] * 4 + [dspec, dspec], [mh, ml, th, tl, a4, spec4]
        name = "hyena_dft_inner_conv"
    return pl.pallas_call(
        body, grid=(hk, d // tc), in_specs=in_specs, out_specs=dspec,
        out_shape=jax.ShapeDtypeStruct(a4.shape, F32), name=name,
        compiler_params=_params(("parallel", "arbitrary")),
    )(*operands)


def _row_tile(m):
    return _tile(m, 1024, 16)


def _short_conv(h, w_in, w_conv, w_out):
    m, d = h.shape
    tm = _row_tile(m)
    tn_in = _tile(d, 256, LANES)
    ncb = d // tn_in
    b, u = _matmul(h, [(w_in, 0), (w_in, ncb), (w_in, 2 * ncb)], [(d, tn_in, F32), (d, tn_in, F32)],
                   _ep_gate_pair, tm=tm, tn=tn_in, name="sc_in_proj")
    t = _sc_gate(b, u, w_conv)
    tn = _tile(d, 512, LANES)
    return _matmul(t, [(w_out, 0)], [(d, tn, F32)], _ep_plain, tm=tm, tn=tn, name="sc_out_proj")[0]


def _ffn(h2, w_gate_up, w_down):
    m, d = h2.shape
    f = w_down.shape[0]
    tm = _row_tile(m)
    tf = _tile(f, 512, LANES)
    a = _matmul(h2, [(w_gate_up, 0), (w_gate_up, f // tf)], [(f, tf, BF16)], _ep_swiglu,
                tm=tm, tn=tf, name="ffn_gate_up")[0]
    tk = f if f <= 4096 else _tile(f, 6144, LANES)
    tn = _tile(d, 512, LANES)
    return _matmul(a, [(w_down, 0)], [(d, tn, F32)], _ep_plain, tm=tm, tn=tn, tk=tk, name="ffn_down")[0]


def _rope_tables(n, n_ctx, scale):
    rows = n // GRID_W
    row = jnp.repeat(jnp.arange(rows, dtype=F32), GRID_W)
    col = jnp.tile(jnp.arange(GRID_W, dtype=F32), rows)
    axis_dim = ROPE_DIM // 2
    inv = ROPE_BASE ** (-jnp.arange(0, axis_dim, 2, dtype=F32) / axis_dim)
    ang_r, ang_c = row[:, None] * inv, col[:, None] * inv
    cos = jnp.concatenate([jnp.cos(ang_r), jnp.cos(ang_c)], axis=1)
    sin = jnp.concatenate([jnp.sin(ang_r), jnp.sin(ang_c)], axis=1)
    rot = jnp.concatenate([cos, cos, -sin, sin], axis=1)
    tab_q = scale * jnp.concatenate([jnp.ones((n, NOPE_DIM), F32), rot], axis=1)
    ctx_rot = jnp.concatenate([jnp.ones((n_ctx, ROPE_DIM), F32), jnp.zeros((n_ctx, ROPE_DIM), F32)], axis=1)
    return tab_q, rot, ctx_rot


def _mla(h, hc, w_down, q_norm, kv_norm, w_uq, w_ukv, w_out):
    n, d = h.shape
    n_ctx = hc.shape[0]
    q_rank, kv_rank = w_uq.shape[0], w_ukv.shape[0]
    heads = w_uq.shape[1] // (NOPE_DIM + ROPE_DIM)
    scale = (NOPE_DIM + ROPE_DIM) ** -0.5
    tab_q, tab_k, tab_kc = _rope_tables(n, n_ctx, scale)

    half = ROPE_DIM // 4
    idx_a = jnp.concatenate([jnp.arange(0, half), jnp.arange(2 * half, 3 * half)])
    idx_b = jnp.concatenate([jnp.arange(half, 2 * half), jnp.arange(3 * half, 4 * half)])
    w_r = w_down[:, q_rank + kv_rank:]
    w_dq = w_down[:, :q_rank].astype(BF16)
    w_dkv = jnp.concatenate([w_down[:, q_rank:q_rank + kv_rank], w_r[:, idx_a], w_r[:, idx_b],
                             w_r[:, idx_b], w_r[:, idx_a]], axis=1).astype(BF16)
    wq3 = w_uq.reshape(q_rank, heads, NOPE_DIM + ROPE_DIM)
    wq_r = wq3[:, :, NOPE_DIM:]
    w_uq_p = jnp.concatenate([wq3[:, :, :NOPE_DIM], wq_r[:, :, idx_a], wq_r[:, :, idx_b],
                              wq_r[:, :, idx_b], wq_r[:, :, idx_a]], axis=2)
    qw = NOPE_DIM + 2 * ROPE_DIM
    w_uq_p = w_uq_p.reshape(q_rank, heads * qw).astype(BF16)
    wkv3 = w_ukv.reshape(kv_rank, heads, NOPE_DIM + V_DIM)
    w_kn = wkv3[:, :, :NOPE_DIM].reshape(kv_rank, heads * NOPE_DIM).astype(BF16)
    w_v = wkv3[:, :, NOPE_DIM:].reshape(kv_rank, heads * V_DIM).astype(BF16)

    tm = _row_tile(n)
    cqn = _matmul(h, [(w_dq, 0)], [(q_rank, q_rank, BF16)], _ep_rms,
                  extras=[(q_norm.reshape(1, q_rank).astype(F32), (1, q_rank), lambda i, j, k: (0, 0))],
                  tm=tm, tn=q_rank, name="mla_q_down")[0]

    def kv_down(hh, tab, name):
        m = hh.shape[0]
        tmk = _row_tile(m)
        wd = kv_rank + 2 * ROPE_DIM
        return _matmul(hh, [(w_dkv, 0)], [(kv_rank, kv_rank, BF16), (LANES, LANES, BF16)],
                       functools.partial(_ep_kv_down, kv_rank=kv_rank),
                       extras=[(kv_norm.reshape(1, kv_rank).astype(F32), (1, kv_rank), lambda i, j, k: (0, 0)),
                               (tab, (tmk, 2 * ROPE_DIM), lambda i, j, k: (i, 0))],
                       tm=tmk, tn=wd, name=name)

    ckv, kr = kv_down(h, tab_k, "mla_kv_down")
    ckv_c, kr_c = kv_down(hc, tab_kc, "mla_kv_down_ctx")
    ckv = jnp.concatenate([ckv, ckv_c], axis=0)
    kr = jnp.concatenate([kr, kr_c], axis=0)
    nk = n + n_ctx

    gq = 2 if heads % 2 == 0 else 1
    q = _matmul(cqn, [(w_uq_p, 0)], [(heads * qw, gq * qw, BF16)], _ep_q_up,
                extras=[(tab_q, (tm, qw), lambda i, j, k: (i, 0))], tm=tm, tn=gq * qw, name="mla_q_up")[0]
    tmk = _tile(nk, 1664, 16)
    k, v = _matmul(ckv, [(w_kn, 0), (w_v, 0)], [(heads * qw, gq * qw, BF16), (heads * V_DIM, gq * V_DIM, BF16)],
                   _ep_kv_up, extras=[(kr, (tmk, LANES), lambda i, j, k: (i, 0))],
                   tm=tmk, tn=gq * NOPE_DIM, name="mla_kv_up")
    o = _attention(q, k, v, heads)
    tn = _tile(d, 512, LANES)
    return _matmul(o, [(w_out, 0)], [(d, tn, F32)], _ep_plain, tm=tm, tn=tn, name="mla_out_proj")[0]


def _hyena(h, w_in, conv_w, conv_b, f_w1, f_b1, f_w2, f_b2, f_w3, f_b3, f_freq, f_w4, skip, w_out):
    n, d = h.shape
    tm = _row_tile(n)
    tn = _tile(d, 512, LANES)
    z = _matmul(h, [(w_in, 0)], [(3 * d, tn, F32)], _ep_plain, tm=tm, tn=tn, name="hy_in_proj")[0]
    x0, vv = _hy_gate(z, conv_w, conv_b)
    n2 = DFT_INNER
    n1 = 2 * n // n2
    hk = n1 // 2
    fwd, inv, mid, mid_t = _dft_tables(n)
    g = _hyena_filter(n, d, f_w1, f_b1, f_w2, f_b2, f_w3, f_b3, f_freq, f_w4)
    tcol = _tile(n2 * d, 4096, LANES)
    ga = _dft_rows(fwd, g.reshape(n1, n2 * d), tcol)
    spec = _dft_mid(mid, ga.reshape(2, hk, n2, d))
    va = _dft_rows(fwd[:, :hk], vv.reshape(hk, n2 * d), tcol)
    vb = _dft_mid(mid, va.reshape(2, hk, n2, d), spec, mid_t)
    t = _dft_rows_inv(inv, vb.reshape(n1, n2 * d), vv.reshape(hk, n2 * d), x0.reshape(hk, n2 * d), skip, d)
    return _matmul(t.reshape(n, d), [(w_out, 0)], [(d, tn, F32)], _ep_plain, tm=tm, tn=tn, name="hy_out_proj")[0]


def kernel(x, c, ctx, c_ctx, ada_down, ada_up, ada_bias, norm_gain, ffn_w_gate_up, ffn_w_down, sc_w_in, sc_conv, sc_w_out, mla_w_down, mla_q_norm, mla_kv_norm, mla_w_uq, mla_w_ukv, mla_w_out, hy_w_in, hy_conv, hy_conv_b, hy_f_w1, hy_f_b1, hy_f_w2, hy_f_b2, hy_f_w3, hy_f_b3, hy_f_freq, hy_f_w4, hy_skip, hy_w_out):
    batch, n, d = x.shape
    assert batch == 1 and c.shape[0] == 1 and ctx.shape[0] == 1
    depth = ada_down.shape[0]
    n_mixers = 3
    xs = x.reshape(n, d)
    cs = ctx.reshape(ctx.shape[1], d)

    mla_layers = [i for i in range(depth) if i % n_mixers == 1]
    last_ctx_read = mla_layers[-1] if mla_layers else -1

    s_raw = jnp.concatenate([c.reshape(1, d), c_ctx.reshape(1, d), jnp.zeros((2 * SUBLANES - 2, d), F32)], axis=0)
    mods = _adaln(s_raw, ada_down, ada_up, ada_bias)

    def mod_vecs(i, row):
        return [mods[i, row, m * d:(m + 1) * d] for m in range(N_MOD)]

    bf = lambda w: w.astype(BF16)
    pend = None
    pend_c = None
    for i in range(depth):
        kind, j = i % n_mixers, i // n_mixers
        ctx_full = i < last_ctx_read
        ctx_keys = i == last_ctx_read
        g = norm_gain[i]
        streams = [(0, xs, pend)]
        if ctx_full or ctx_keys:
            streams.append((1, cs, pend_c))
        hs = {}
        cur = {}
        for row, xv, pd in streams:
            mv = mod_vecs(i, row)
            if pd is None:
                _, hh = _resid_norm_mod(xv, None, _vec_rows(d, g[0], g[0], g[0], mv[0], mv[1]), has_h=True)
            else:
                xv, hh = _resid_norm_mod(xv, pd[0], _vec_rows(d, pd[1], pd[2], g[0], mv[0], mv[1]), has_h=True)
            hs[row], cur[row] = hh, xv

        ys = {}
        if kind == 0:
            w_in, w_out = bf(sc_w_in[j]), bf(sc_w_out[j])
            ys[0] = _short_conv(hs[0], w_in, sc_conv[j], w_out)
            if ctx_full:
                ys[1] = _short_conv(hs[1], w_in, sc_conv[j], w_out)
        elif kind == 1:
            ys[0] = _mla(hs[0], hs[1], mla_w_down[j], mla_q_norm[j], mla_kv_norm[j], mla_w_uq[j], mla_w_ukv[j],
                         bf(mla_w_out[j]))
            assert not ctx_full
        else:
            hp = (bf(hy_w_in[j]), hy_conv[j], hy_conv_b[j], hy_f_w1[j], hy_f_b1[j], hy_f_w2[j], hy_f_b2[j],
                  hy_f_w3[j], hy_f_b3[j], hy_f_freq[j], hy_f_w4[j], hy_skip[j], bf(hy_w_out[j]))
            ys[0] = _hyena(hs[0], *hp)
            if ctx_full:
                ys[1] = _hyena(hs[1], *hp)

        w_gu, w_dn = bf(ffn_w_gate_up[i]), bf(ffn_w_down[i])
        new_pend = {0: None, 1: None}
        for row in ys:
            mv = mod_vecs(i, row)
            xv, h2 = _resid_norm_mod(cur[row], ys[row], _vec_rows(d, mv[2], g[1], g[2], mv[3], mv[4]), has_h=True)
            cur[row] = xv
            new_pend[row] = (_ffn(h2, w_gu, w_dn), mv[5], g[3])
        xs, pend = cur[0], new_pend[0]
        if ctx_full:
            cs, pend_c = cur[1], new_pend[1]
        else:
            pend_c = None

    xs, _ = _resid_norm_mod(xs, pend[0], _vec_rows(d, pend[1], pend[2], pend[2], pend[1], pend[1]), has_h=False)
    return xs.reshape(batch, n, d)
```

```python
import functools
import math

import jax
import jax.numpy as jnp
from jax import lax
from jax.experimental import pallas as pl
from jax.experimental.pallas import tpu as pltpu

F32 = jnp.float32
BF16 = jnp.bfloat16

EPS = 1e-6
N_MOD = 6
NOPE_DIM = 128
ROPE_DIM = 64
V_DIM = 128
GRID_W = 64
ROPE_BASE = 10000.0
HY_BANDS = 16
HY_DECAY_TARGET = 1e-2
HY_FAST_DECAY_PCT = 0.3
HY_SLOW_DECAY_PCT = 1.5

LANES = 128
SUBLANES = 8
VMEM_LIMIT_BYTES = 56 * 1024 * 1024
DFT_INNER = 128
HALO_ROWS = 16
MXU_DEPTH = 256


def _tile(dim, pref, align):
    best = None
    t = align
    while t <= min(dim, pref):
        if dim % t == 0:
            best = t
        t += align
    return best if best is not None else dim


def _params(sem):
    return pltpu.CompilerParams(dimension_semantics=sem, vmem_limit_bytes=VMEM_LIMIT_BYTES)


def _split_hi_lo(x):
    hi = x.astype(BF16)
    lo = (x - hi.astype(F32)).astype(BF16)
    return hi, lo


def _dot(a, b):
    return jnp.dot(a, b, preferred_element_type=F32)


def _dot3(a, b):
    ah, al = _split_hi_lo(a)
    bh, bl = _split_hi_lo(b)
    return _dot(ah, bh) + _dot(ah, bl) + _dot(al, bh)


def _dot3_pre(ah, al, b):
    bh, bl = _split_hi_lo(b)
    return _dot(ah, bh) + _dot(ah, bl) + _dot(al, bh)


def _rms(x, gain):
    return x * lax.rsqrt(jnp.mean(x * x, axis=-1, keepdims=True) + EPS) * gain


def _silu(x):
    return x * (1.0 / (1.0 + jnp.exp(-x)))


def _mm_body(*refs, n_w, n_extra, n_out, nk, epilogue):
    a_ref = refs[0]
    w_refs = refs[1:1 + n_w]
    extra_refs = refs[1 + n_w:1 + n_w + n_extra]
    out_refs = refs[1 + n_w + n_extra:1 + n_w + n_extra + n_out]
    acc_refs = refs[1 + n_w + n_extra + n_out:]
    a = a_ref[...].astype(BF16)
    dots = [_dot(a, w[...].astype(BF16)) for w in w_refs]
    if nk == 1:
        epilogue(dots, extra_refs, out_refs)
        return
    k = pl.program_id(2)

    @pl.when(k == 0)
    def _():
        for acc, d in zip(acc_refs, dots):
            acc[...] = d

    @pl.when(k > 0)
    def _():
        for acc, d in zip(acc_refs, dots):
            acc[...] += d

    @pl.when(k == nk - 1)
    def _():
        epilogue([acc[...] for acc in acc_refs], extra_refs, out_refs)


def _matmul(a, ws, outs, epilogue, *, extras=(), tm, tn, tk=None, name):
    M, K = a.shape
    tk = K if tk is None else tk
    nk = K // tk
    n_col_blocks = outs[0][0] // outs[0][1]
    grid = (M // tm, n_col_blocks, nk)
    in_specs = [pl.BlockSpec((tm, tk), lambda i, j, k: (i, k))]
    operands = [a]
    for w, off in ws:
        if isinstance(w, tuple):
            w, layer = w
            in_specs.append(pl.BlockSpec((None, tk, tn),
                                         functools.partial(lambda i, j, k, o, l: (l, k, j + o), o=off, l=layer)))
        else:
            in_specs.append(pl.BlockSpec((tk, tn), functools.partial(lambda i, j, k, o: (k, j + o), o=off)))
        operands.append(w)
    for arr, bshape, imap in extras:
        in_specs.append(pl.BlockSpec(bshape, imap))
        operands.append(arr)
    out_shape = [jax.ShapeDtypeStruct((M, wt), dt) for wt, _, dt in outs]
    out_specs = [pl.BlockSpec((tm, bw), lambda i, j, k: (i, j)) for _, bw, _ in outs]
    scratch = [pltpu.VMEM((tm, tn), F32) for _ in ws] if nk > 1 else []
    body = functools.partial(_mm_body, n_w=len(ws), n_extra=len(extras), n_out=len(outs), nk=nk,
                             epilogue=epilogue)
    res = pl.pallas_call(
        body, grid=grid, in_specs=in_specs, out_specs=out_specs, out_shape=out_shape,
        scratch_shapes=scratch, name=name,
        compiler_params=_params(("parallel", "arbitrary", "arbitrary")),
    )(*operands)
    return res


def _ep_plain(dots, extras, outs):
    outs[0][...] = dots[0].astype(outs[0].dtype)


def _ep_gate_pair(dots, extras, outs):
    outs[0][...] = dots[0].astype(outs[0].dtype)
    outs[1][...] = (dots[1] * dots[2]).astype(outs[1].dtype)


def _ep_swiglu(dots, extras, outs):
    outs[0][...] = (_silu(dots[0]) * dots[1]).astype(outs[0].dtype)


def _ep_rms(dots, extras, outs):
    outs[0][...] = _rms(dots[0], extras[0][...]).astype(outs[0].dtype)


def _ep_kv_down(dots, extras, outs, *, kv_rank):
    gain_ref, tab_ref = extras
    d = dots[0]
    outs[0][...] = _rms(d[:, :kv_rank], gain_ref[...]).astype(outs[0].dtype)
    p = d[:, kv_rank:] * tab_ref[...]
    outs[1][...] = (p + pltpu.roll(p, ROPE_DIM, axis=1)).astype(outs[1].dtype)


def _ep_q_up(dots, extras, outs):
    tab = extras[0][...]
    x = dots[0]
    width = tab.shape[1]
    for g in range(x.shape[1] // width):
        outs[0][:, g * width:(g + 1) * width] = (x[:, g * width:(g + 1) * width] * tab).astype(outs[0].dtype)


def _ep_k_up(dots, extras, outs):
    kn = dots[0]
    kr = extras[0][...]
    for g in range(kn.shape[1] // NOPE_DIM):
        base = g * (NOPE_DIM + LANES)
        outs[0][:, base:base + NOPE_DIM] = kn[:, g * NOPE_DIM:(g + 1) * NOPE_DIM].astype(outs[0].dtype)
        outs[0][:, base + NOPE_DIM:base + NOPE_DIM + LANES] = kr


def _adaln_body(s_ref, down_ref, up_ref, bias_ref, out_ref, t_ref):
    @pl.when(pl.program_id(1) == 0)
    def _():
        t_ref[...] = _dot3(_silu(s_ref[...]), down_ref[0])

    out_ref[0] = _dot3(t_ref[...], up_ref[0]) + bias_ref[0]


def _adaln(s_raw, ada_down, ada_up, ada_bias):
    depth, d, r = ada_down.shape
    n6 = ada_up.shape[2]
    rows = s_raw.shape[0]
    tn = _tile(n6, 2048, LANES)
    return pl.pallas_call(
        _adaln_body, grid=(depth, n6 // tn),
        in_specs=[pl.BlockSpec((rows, d), lambda i, j: (0, 0)),
                  pl.BlockSpec((1, d, r), lambda i, j: (i, 0, 0)),
                  pl.BlockSpec((1, r, tn), lambda i, j: (i, 0, j)),
                  pl.BlockSpec((1, 1, tn), lambda i, j: (i, 0, j))],
        out_specs=pl.BlockSpec((1, rows, tn), lambda i, j: (i, 0, j)),
        out_shape=jax.ShapeDtypeStruct((depth, rows, n6), F32),
        scratch_shapes=[pltpu.VMEM((rows, r), F32)], name="adaln",
        compiler_params=_params(("arbitrary", "arbitrary")),
    )(s_raw, ada_down, ada_up, ada_bias.reshape(depth, 1, n6))


def _rnm_body(*refs, has_resid, has_h):
    refs = list(refs)
    x_ref = refs.pop(0)
    y_ref = refs.pop(0) if has_resid else None
    vec_ref = refs.pop(0)
    x = x_ref[...]
    if has_resid:
        xo_ref = refs.pop(0)
        x = x + vec_ref[0:1, :] * _rms(y_ref[...].astype(F32), vec_ref[1:2, :])
        xo_ref[...] = x
    if has_h:
        h_ref = refs.pop(0)
        h_ref[...] = (_rms(x, vec_ref[2:3, :]) * (1.0 + vec_ref[4:5, :]) + vec_ref[3:4, :]).astype(h_ref.dtype)


def _resid_norm_mod(x, y, vec, *, has_h, rows=None):
    n, d = x.shape
    n = n if rows is None else rows
    has_resid = y is not None
    tr = _tile(n, 256, 16)
    spec = pl.BlockSpec((tr, d), lambda i: (i, 0))
    in_specs = [spec] + ([spec] if has_resid else []) + [pl.BlockSpec(vec.shape, lambda i: (0, 0))]
    operands = [x] + ([y] if has_resid else []) + [vec]
    out_shape, out_specs = [], []
    if has_resid:
        out_shape.append(jax.ShapeDtypeStruct((n, d), F32))
        out_specs.append(spec)
    if has_h:
        out_shape.append(jax.ShapeDtypeStruct((n, d), BF16))
        out_specs.append(spec)
    res = pl.pallas_call(
        functools.partial(_rnm_body, has_resid=has_resid, has_h=has_h), grid=(n // tr,),
        in_specs=in_specs, out_specs=out_specs, out_shape=out_shape, name="resid_norm_mod",
        compiler_params=_params(("parallel",)),
    )(*operands)
    res = list(res)
    x_new = res.pop(0) if has_resid else None
    h = res.pop(0) if has_h else None
    return x_new, h


def _vec_rows(d, *rows):
    out = [r.reshape(1, d).astype(F32) for r in rows]
    out += [jnp.zeros((1, d), F32)] * (SUBLANES - len(out))
    return jnp.concatenate(out, axis=0)


def _conv3_rows(main, prev_blk, next_blk, w0, w1, w2, is_first, is_last):
    main = main.astype(F32)
    tr = main.shape[0]
    rows = lax.broadcasted_iota(jnp.int32, main.shape, 0)
    prev_row = jnp.where(is_first, 0.0, prev_blk[HALO_ROWS - 1:HALO_ROWS, :].astype(F32))
    next_row = jnp.where(is_last, 0.0, next_blk[0:1, :].astype(F32))
    up = jnp.where(rows == 0, prev_row, pltpu.roll(main, 1, axis=0))
    dn = jnp.where(rows == tr - 1, next_row, pltpu.roll(main, tr - 1, axis=0))
    return w0 * up + w1 * main + w2 * dn


def _halo_specs(tr, tc, n_rows, col_off):
    per = tr // HALO_ROWS
    last = n_rows // HALO_ROWS - 1
    return [
        pl.BlockSpec((tr, tc), lambda i, j: (i, j + col_off)),
        pl.BlockSpec((HALO_ROWS, tc), lambda i, j: (jnp.maximum(i * per - 1, 0), j + col_off)),
        pl.BlockSpec((HALO_ROWS, tc), lambda i, j: (jnp.minimum((i + 1) * per, last), j + col_off)),
    ]


def _sc_gate_body(b_ref, u_ref, up_ref, un_ref, w_ref, o_ref):
    i = pl.program_id(0)
    conv = _conv3_rows(u_ref[...], up_ref[...], un_ref[...], w_ref[0:1, :], w_ref[1:2, :], w_ref[2:3, :],
                       i == 0, i == pl.num_programs(0) - 1)
    o_ref[...] = (b_ref[...].astype(F32) * conv).astype(o_ref.dtype)


def _sc_gate(b, u, w_conv):
    n, d = u.shape
    tr = _tile(n, 512, 16)
    tc = _tile(d, 1024, LANES)
    w8 = jnp.concatenate([w_conv.astype(F32), jnp.zeros((SUBLANES - 3, d), F32)], axis=0)
    return pl.pallas_call(
        _sc_gate_body, grid=(n // tr, d // tc),
        in_specs=[pl.BlockSpec((tr, tc), lambda i, j: (i, j))] + _halo_specs(tr, tc, n, 0)
        + [pl.BlockSpec((SUBLANES, tc), lambda i, j: (0, j))],
        out_specs=pl.BlockSpec((tr, tc), lambda i, j: (i, j)),
        out_shape=jax.ShapeDtypeStruct((n, d), BF16), name="sc_conv_gate",
        compiler_params=_params(("parallel", "parallel")),
    )(b, u, u, u, w8)


def _hy_gate_body(*refs):
    z_refs = refs[0:9]
    w_refs = refs[9:12]
    x0_ref, vv_ref = refs[12:14]
    i = pl.program_id(0)
    first, last = i == 0, i == pl.num_programs(0) - 1
    conv = []
    for g in range(3):
        m, p, nx = z_refs[3 * g:3 * g + 3]
        w = w_refs[g]
        conv.append(_conv3_rows(m[...], p[...], nx[...], w[0:1, :], w[1:2, :], w[2:3, :], first, last) + w[3:4, :])
    x0_ref[...] = conv[0]
    vv_ref[...] = conv[1] * conv[2]


def _hy_gate(z, conv_w, conv_b):
    n, d3 = z.shape
    d = d3 // 3
    tr = _tile(n, 512, 16)
    tc = _tile(d, 512, LANES)
    ncb = d // tc
    w8 = jnp.concatenate([conv_w.astype(F32), conv_b.reshape(1, d3).astype(F32),
                          jnp.zeros((SUBLANES - 4, d3), F32)], axis=0)
    in_specs, operands = [], []
    for g in range(3):
        in_specs += _halo_specs(tr, tc, n, g * ncb)
        operands += [z, z, z]
    for g in range(3):
        in_specs.append(pl.BlockSpec((SUBLANES, tc), functools.partial(lambda i, j, o: (0, j + o), o=g * ncb)))
        operands.append(w8)
    spec = pl.BlockSpec((tr, tc), lambda i, j: (i, j))
    return pl.pallas_call(
        _hy_gate_body, grid=(n // tr, ncb), in_specs=in_specs, out_specs=[spec, spec],
        out_shape=[jax.ShapeDtypeStruct((n, d), F32)] * 2, name="hy_conv_gate",
        compiler_params=_params(("parallel", "parallel")),
    )(*operands)


def _attn_body(q_ref, k_ref, vt_ref, o_ref, m_ref, l_ref, acc_ref, *, ck):
    ki = pl.program_id(2)

    @pl.when(ki == 0)
    def _():
        m_ref[...] = jnp.full(m_ref.shape, -jnp.inf, F32)
        l_ref[...] = jnp.zeros(l_ref.shape, F32)
        acc_ref[...] = jnp.zeros(acc_ref.shape, F32)

    q = q_ref[...]
    m, l, acc = m_ref[...], l_ref[...], acc_ref[...]
    tk = k_ref.shape[0]
    bounds = [(lo, min(lo + ck, tk)) for lo in range(0, tk, ck)]

    def scores(b):
        return lax.dot_general(k_ref[b[0]:b[1], :], q, (((1,), (1,)), ((), ())), preferred_element_type=F32)

    s_next = scores(bounds[0])
    pending = None
    for c, b in enumerate(bounds):
        s = s_next
        if c + 1 < len(bounds):
            s_next = scores(bounds[c + 1])
        if pending is not None:
            a_prev, p_prev, b_prev = pending
            acc = a_prev * acc + _dot(vt_ref[:, b_prev[0]:b_prev[1]], p_prev)
        m_new = jnp.maximum(m, jnp.max(s, axis=0, keepdims=True))
        alpha = jnp.exp2(m - m_new)
        p = jnp.exp2(s - m_new)
        l = alpha * l + jnp.sum(p, axis=0, keepdims=True)
        pending = (alpha, p.astype(BF16), b)
        m = m_new
    a_prev, p_prev, b_prev = pending
    acc = a_prev * acc + _dot(vt_ref[:, b_prev[0]:b_prev[1]], p_prev)
    m_ref[...], l_ref[...], acc_ref[...] = m, l, acc

    @pl.when(ki == pl.num_programs(2) - 1)
    def _():
        o_ref[...] = (acc * (1.0 / l)).T.astype(o_ref.dtype)


def _attention(q, k, vt, heads):
    n = q.shape[0]
    nk = k.shape[0]
    qw = q.shape[1] // heads
    unit = MXU_DEPTH if nk % MXU_DEPTH == 0 else LANES
    ck = 3 * MXU_DEPTH
    tq = _tile(n, 1024, LANES)
    tk = _tile(nk, 13 * MXU_DEPTH, unit)
    return pl.pallas_call(
        functools.partial(_attn_body, ck=ck), grid=(heads, n // tq, nk // tk),
        in_specs=[pl.BlockSpec((tq, qw), lambda h, i, j: (i, h)),
                  pl.BlockSpec((tk, qw), lambda h, i, j: (j, h)),
                  pl.BlockSpec((V_DIM, tk), lambda h, i, j: (h, j))],
        out_specs=pl.BlockSpec((tq, V_DIM), lambda h, i, j: (i, h)),
        out_shape=jax.ShapeDtypeStruct((n, heads * V_DIM), BF16),
        scratch_shapes=[pltpu.VMEM((1, tq), F32), pltpu.VMEM((1, tq), F32), pltpu.VMEM((V_DIM, tq), F32)],
        name="mla_flash_attention",
        compiler_params=_params(("parallel", "parallel", "arbitrary")),
    )(q, k, vt)


def _vt_body(w_ref, c_ref, o_ref):
    o_ref[...] = lax.dot_general(w_ref[...], c_ref[...], (((1,), (1,)), ((), ())),
                                 preferred_element_type=F32).astype(o_ref.dtype)


def _v_up_transposed(w_vt, ckv):
    hv, r = w_vt.shape
    nk = ckv.shape[0]
    th = _tile(hv, 2 * V_DIM, V_DIM)
    tn = _tile(nk, 13 * MXU_DEPTH, LANES)
    return pl.pallas_call(
        _vt_body, grid=(hv // th, nk // tn),
        in_specs=[pl.BlockSpec((th, r), lambda i, j: (i, 0)), pl.BlockSpec((tn, r), lambda i, j: (j, 0))],
        out_specs=pl.BlockSpec((th, tn), lambda i, j: (i, j)),
        out_shape=jax.ShapeDtypeStruct((hv, nk), BF16), name="mla_v_up_transposed",
        compiler_params=_params(("parallel", "parallel")),
    )(w_vt, ckv)


def _filter_body(z_ref, w1_ref, w2_ref, w3_ref, bf_ref, w4_ref, dl_ref, o_ref, h_ref):
    @pl.when(pl.program_id(1) == 0)
    def _():
        bf = bf_ref[...]
        h = jnp.sin(bf[3:4, :] * (_dot3(z_ref[...], w1_ref[...]) + bf[0:1, :]))
        h = jnp.sin(bf[4:5, :] * (_dot3(h, w2_ref[...]) + bf[1:2, :]))
        h_ref[...] = jnp.sin(bf[5:6, :] * (_dot3(h, w3_ref[...]) + bf[2:3, :]))

    z = z_ref[...]
    t = z[:, 0:1]
    sign = z[:, HY_BANDS * 2 + 1:HY_BANDS * 2 + 2]
    filt = _dot3(h_ref[...], w4_ref[...])
    o_ref[...] = sign * filt * jnp.exp(-t * dl_ref[...])


def _hyena_filter(n, d, f_w1, f_b1, f_w2, f_b2, f_w3, f_b3, f_freq, f_w4):
    fh = f_w1.shape[1]
    emb = f_w1.shape[0]
    r = jnp.arange(2 * n, dtype=jnp.int32)
    p = jnp.minimum(jnp.where(r < n, r, 2 * n - r), n - 1)
    t = jnp.linspace(0.0, 1.0, n, dtype=F32)[p][:, None]
    w = ((2.0 * math.pi / n) * jnp.arange(n, dtype=F32))[p][:, None]
    bands = jnp.linspace(1e-4, HY_BANDS - 1, HY_BANDS, dtype=F32)
    sign = jnp.where(r < n, 1.0, jnp.where(r == n, 0.0, -1.0)).astype(F32)[:, None]
    zw = LANES // 2
    z = jnp.concatenate([t, jnp.cos(bands * w), -jnp.sin(bands * w), sign,
                         jnp.zeros((2 * n, zw - emb - 1), F32)], axis=-1)
    w1p = jnp.concatenate([f_w1.astype(F32), jnp.zeros((zw - emb, fh), F32)], axis=0)
    bf = jnp.concatenate([f_b1.reshape(1, fh), f_b2.reshape(1, fh), f_b3.reshape(1, fh),
                          f_freq.reshape(3, fh), jnp.zeros((2, fh), F32)], axis=0).astype(F32)
    deltas = jnp.abs(jnp.linspace(math.log(HY_DECAY_TARGET) / HY_SLOW_DECAY_PCT,
                                  math.log(HY_DECAY_TARGET) / HY_FAST_DECAY_PCT, d, dtype=F32)).reshape(1, d)
    tr = _tile(n, 512, SUBLANES)
    tc = _tile(d, 1024, LANES)
    ncb = d // tc
    half = n // tr
    const = lambda shape: pl.BlockSpec(shape, lambda i, j: (0, 0))
    return pl.pallas_call(
        _filter_body, grid=(2 * n // tr, ncb),
        in_specs=[pl.BlockSpec((tr, zw), lambda i, j: (i, 0)), const((zw, fh)), const((fh, fh)), const((fh, fh)),
                  const((SUBLANES, fh)),
                  pl.BlockSpec((fh, tc), lambda i, j: (0, j + jnp.where(i >= half, ncb, 0))),
                  pl.BlockSpec((1, tc), lambda i, j: (0, j))],
        out_specs=pl.BlockSpec((tr, tc), lambda i, j: (i, j)),
        out_shape=jax.ShapeDtypeStruct((2 * n, d), F32),
        scratch_shapes=[pltpu.VMEM((tr, fh), F32)], name="hyena_filter",
        compiler_params=_params(("parallel", "arbitrary")),
    )(z, w1p, f_w2.astype(F32), f_w3.astype(F32), bf, f_w4.astype(F32), deltas)


def _dft_tables(n):
    big_n = 2 * n
    n2 = DFT_INNER
    n1 = big_n // n2
    hk = n1 // 2
    k1 = jnp.arange(hk, dtype=jnp.int32)
    m1 = jnp.arange(n1, dtype=jnp.int32)
    ang_a = (math.pi / n1) * ((m1[None, :] * (2 * k1[:, None] + 1)) % (2 * n1)).astype(F32)
    fwd = jnp.concatenate([jnp.cos(ang_a), -jnp.sin(ang_a)], axis=0)
    inv = (2.0 / big_n) * jnp.concatenate([jnp.cos(ang_a[:, :hk]).T, -jnp.sin(ang_a[:, :hk]).T], axis=1)
    k2 = jnp.arange(n2, dtype=jnp.int32)
    m2 = jnp.arange(n2, dtype=jnp.int32)
    freq = 2 * k1[:, None, None] + 1 + 2 * n1 * k2[None, :, None]
    ang_c = (math.pi / big_n) * ((m2[None, None, :] * freq) % (2 * big_n)).astype(F32)
    gr, gi = jnp.cos(ang_c), -jnp.sin(ang_c)
    mid = jnp.concatenate([jnp.concatenate([gr, -gi], axis=2), jnp.concatenate([gi, gr], axis=2)], axis=1)
    mid_t = jnp.swapaxes(mid, 1, 2)
    return fwd, inv, mid, mid_t


def _dft_rows_body(fh_ref, fl_ref, x_ref, o_ref):
    o_ref[...] = _dot3_pre(fh_ref[...], fl_ref[...], x_ref[...])


def _dft_rows(f, x2d, tn):
    r, k = f.shape
    c = x2d.shape[1]
    fh, fl = _split_hi_lo(f)
    return pl.pallas_call(
        _dft_rows_body, grid=(c // tn,),
        in_specs=[pl.BlockSpec((r, k), lambda j: (0, 0)), pl.BlockSpec((r, k), lambda j: (0, 0)),
                  pl.BlockSpec((k, tn), lambda j: (0, j))],
        out_specs=pl.BlockSpec((r, tn), lambda j: (0, j)),
        out_shape=jax.ShapeDtypeStruct((r, c), F32), name="hyena_dft_outer",
        compiler_params=_params(("parallel",)),
    )(fh, fl, x2d)


def _dft_rows_inv_body(fh_ref, fl_ref, b_ref, v_ref, x0_ref, skip_ref, o_ref):
    y = _dot3_pre(fh_ref[...], fl_ref[...], b_ref[...])
    v = v_ref[...]
    o_ref[...] = (x0_ref[...] * (y + v * skip_ref[...])).astype(o_ref.dtype)


def _dft_rows_inv(f, b2d, v2d, x02d, skip, d):
    r, k = f.shape
    c = b2d.shape[1]
    fh, fl = _split_hi_lo(f)
    const = pl.BlockSpec((r, k), lambda j: (0, 0))
    col = pl.BlockSpec((r, d), lambda j: (0, j))
    return pl.pallas_call(
        _dft_rows_inv_body, grid=(c // d,),
        in_specs=[const, const, pl.BlockSpec((k, d), lambda j: (0, j)), col, col,
                  pl.BlockSpec((1, d), lambda j: (0, 0))],
        out_specs=col, out_shape=jax.ShapeDtypeStruct((r, c), BF16), name="hyena_dft_outer_inverse",
        compiler_params=_params(("parallel",)),
    )(fh, fl, b2d, v2d, x02d, skip.reshape(1, d).astype(F32))


def _mid_apply(mh_ref, ml_ref, re, im):
    n2 = re.shape[0]
    mh, ml = mh_ref[0], ml_ref[0]
    return (_dot3_pre(mh[:, :n2], ml[:, :n2], re) + _dot3_pre(mh[:, n2:], ml[:, n2:], im))


def _dft_mid_body(mh_ref, ml_ref, a_ref, o_ref):
    n2 = a_ref.shape[2]
    s = _mid_apply(mh_ref, ml_ref, a_ref[0, 0], a_ref[1, 0])
    o_ref[0, 0] = s[:n2]
    o_ref[1, 0] = s[n2:]


def _dft_conv_body(mh_ref, ml_ref, th_ref, tl_ref, a_ref, g_ref, o_ref):
    n2 = a_ref.shape[2]
    s = _mid_apply(mh_ref, ml_ref, a_ref[0, 0], a_ref[1, 0])
    sr, si = s[:n2], s[n2:]
    gr, gi = g_ref[0, 0], g_ref[1, 0]
    b = _mid_apply(th_ref, tl_ref, sr * gr - si * gi, sr * gi + si * gr)
    o_ref[0, 0] = b[:n2]
    o_ref[1, 0] = b[n2:]


def _dft_mid(mid, a4, spec4=None, mid_t=None):
    _, hk, n2, d = a4.shape
    tc = _tile(d, 1024, LANES)
    mspec = pl.BlockSpec((1, 2 * n2, 2 * n2), lambda k, j: (k, 0, 0))
    dspec = pl.BlockSpec((2, 1, n2, tc), lambda k, j: (0, k, 0, j))
    mh, ml = _split_hi_lo(mid)
    if spec4 is None:
        body, in_specs, operands, name = _dft_mid_body, [mspec, mspec, dspec], [mh, ml, a4], "hyena_dft_inner"
    else:
        th, tl = _split_hi_lo(mid_t)
        body, in_specs, operands = _dft_conv_body, [mspec] * 4 + [dspec, dspec], [mh, ml, th, tl, a4, spec4]
        name = "hyena_dft_inner_conv"
    return pl.pallas_call(
        body, grid=(hk, d // tc), in_specs=in_specs, out_specs=dspec,
        out_shape=jax.ShapeDtypeStruct(a4.shape, F32), name=name,
        compiler_params=_params(("parallel", "arbitrary")),
    )(*operands)


def _row_tile(m):
    return _tile(m, 1024, 16)


def _short_conv(h, w_in, w_conv, w_out):
    m, d = h.shape
    tm = _row_tile(m)
    tn_in = _tile(d, 256, LANES)
    ncb = d // tn_in
    b, u = _matmul(h, [(w_in, 0), (w_in, ncb), (w_in, 2 * ncb)], [(d, tn_in, BF16), (d, tn_in, BF16)],
                   _ep_gate_pair, tm=tm, tn=tn_in, name="sc_in_proj")
    t = _sc_gate(b, u, w_conv)
    tn = _tile(d, 512, LANES)
    return _matmul(t, [(w_out, 0)], [(d, tn, BF16)], _ep_plain, tm=tm, tn=tn, name="sc_out_proj")[0]


def _ffn(h2, w_gate_up, w_down):
    m, d = h2.shape
    f = w_down[0].shape[1]
    tm = _row_tile(m)
    tf = _tile(f, 512, LANES)
    a = _matmul(h2, [(w_gate_up, 0), (w_gate_up, f // tf)], [(f, tf, BF16)], _ep_swiglu,
                tm=tm, tn=tf, name="ffn_gate_up")[0]
    tk = f if f <= 4096 else _tile(f, 6144, LANES)
    tn = _tile(d, 512, LANES)
    return _matmul(a, [(w_down, 0)], [(d, tn, BF16)], _ep_plain, tm=tm, tn=tn, tk=tk, name="ffn_down")[0]


def _rope_tables(n, n_ctx, scale):
    rows = n // GRID_W
    row = jnp.repeat(jnp.arange(rows, dtype=F32), GRID_W)
    col = jnp.tile(jnp.arange(GRID_W, dtype=F32), rows)
    axis_dim = ROPE_DIM // 2
    inv = ROPE_BASE ** (-jnp.arange(0, axis_dim, 2, dtype=F32) / axis_dim)
    ang_r, ang_c = row[:, None] * inv, col[:, None] * inv
    cos = jnp.concatenate([jnp.cos(ang_r), jnp.cos(ang_c)], axis=1)
    sin = jnp.concatenate([jnp.sin(ang_r), jnp.sin(ang_c)], axis=1)
    rot = jnp.concatenate([cos, cos, -sin, sin], axis=1)
    tab_q = scale * jnp.concatenate([jnp.ones((n, NOPE_DIM), F32), rot], axis=1)
    ctx_rot = jnp.concatenate([jnp.ones((n_ctx, ROPE_DIM), F32), jnp.zeros((n_ctx, ROPE_DIM), F32)], axis=1)
    return tab_q, rot, ctx_rot


def _mla(h, hc, w_down, q_norm, kv_norm, w_uq, w_ukv, w_out):
    n, d = h.shape
    n_ctx = hc.shape[0]
    q_rank, kv_rank = w_uq.shape[0], w_ukv.shape[0]
    heads = w_uq.shape[1] // (NOPE_DIM + ROPE_DIM)
    scale = (NOPE_DIM + ROPE_DIM) ** -0.5 * math.log2(math.e)
    tab_q, tab_k, tab_kc = _rope_tables(n, n_ctx, scale)

    half = ROPE_DIM // 4
    idx_a = jnp.concatenate([jnp.arange(0, half), jnp.arange(2 * half, 3 * half)])
    idx_b = jnp.concatenate([jnp.arange(half, 2 * half), jnp.arange(3 * half, 4 * half)])
    w_r = w_down[:, q_rank + kv_rank:]
    w_dq = w_down[:, :q_rank].astype(BF16)
    w_dkv = jnp.concatenate([w_down[:, q_rank:q_rank + kv_rank], w_r[:, idx_a], w_r[:, idx_b],
                             w_r[:, idx_b], w_r[:, idx_a]], axis=1).astype(BF16)
    wq3 = w_uq.reshape(q_rank, heads, NOPE_DIM + ROPE_DIM)
    wq_r = wq3[:, :, NOPE_DIM:]
    w_uq_p = jnp.concatenate([wq3[:, :, :NOPE_DIM], wq_r[:, :, idx_a], wq_r[:, :, idx_b],
                              wq_r[:, :, idx_b], wq_r[:, :, idx_a]], axis=2)
    qw = NOPE_DIM + 2 * ROPE_DIM
    w_uq_p = w_uq_p.reshape(q_rank, heads * qw).astype(BF16)
    wkv3 = w_ukv.reshape(kv_rank, heads, NOPE_DIM + V_DIM)
    w_kn = wkv3[:, :, :NOPE_DIM].reshape(kv_rank, heads * NOPE_DIM).astype(BF16)
    w_vt = wkv3[:, :, NOPE_DIM:].reshape(kv_rank, heads * V_DIM).T.astype(BF16)

    tm = _row_tile(n)
    cqn = _matmul(h, [(w_dq, 0)], [(q_rank, q_rank, BF16)], _ep_rms,
                  extras=[(q_norm.reshape(1, q_rank).astype(F32), (1, q_rank), lambda i, j, k: (0, 0))],
                  tm=tm, tn=q_rank, name="mla_q_down")[0]

    def kv_down(hh, tab, name):
        m = hh.shape[0]
        tmk = _row_tile(m)
        wd = kv_rank + 2 * ROPE_DIM
        return _matmul(hh, [(w_dkv, 0)], [(kv_rank, kv_rank, BF16), (LANES, LANES, BF16)],
                       functools.partial(_ep_kv_down, kv_rank=kv_rank),
                       extras=[(kv_norm.reshape(1, kv_rank).astype(F32), (1, kv_rank), lambda i, j, k: (0, 0)),
                               (tab, (tmk, 2 * ROPE_DIM), lambda i, j, k: (i, 0))],
                       tm=tmk, tn=wd, name=name)

    ckv, kr = kv_down(h, tab_k, "mla_kv_down")
    ckv_c, kr_c = kv_down(hc, tab_kc, "mla_kv_down_ctx")
    ckv = jnp.concatenate([ckv, ckv_c], axis=0)
    kr = jnp.concatenate([kr, kr_c], axis=0)
    nk = n + n_ctx

    gq = 2 if heads % 2 == 0 else 1
    q = _matmul(cqn, [(w_uq_p, 0)], [(heads * qw, gq * qw, BF16)], _ep_q_up,
                extras=[(tab_q, (tm, qw), lambda i, j, k: (i, 0))], tm=tm, tn=gq * qw, name="mla_q_up")[0]
    tmk = _tile(nk, 1664, 16)
    k = _matmul(ckv, [(w_kn, 0)], [(heads * qw, gq * qw, BF16)], _ep_k_up,
                extras=[(kr, (tmk, LANES), lambda i, j, k: (i, 0))], tm=tmk, tn=gq * NOPE_DIM, name="mla_k_up")[0]
    vt = _v_up_transposed(w_vt, ckv)
    o = _attention(q, k, vt, heads)
    tn = _tile(d, 512, LANES)
    return _matmul(o, [(w_out, 0)], [(d, tn, BF16)], _ep_plain, tm=tm, tn=tn, name="mla_out_proj")[0]


def _hyena(h, w_in, conv_w, conv_b, f_w1, f_b1, f_w2, f_b2, f_w3, f_b3, f_freq, f_w4, skip, w_out):
    n, d = h.shape
    tm = _row_tile(n)
    tn = _tile(d, 512, LANES)
    z = _matmul(h, [(w_in, 0)], [(3 * d, tn, BF16)], _ep_plain, tm=tm, tn=tn, name="hy_in_proj")[0]
    x0, vv = _hy_gate(z, conv_w, conv_b)
    n2 = DFT_INNER
    n1 = 2 * n // n2
    hk = n1 // 2
    fwd, inv, mid, mid_t = _dft_tables(n)
    g = _hyena_filter(n, d, f_w1, f_b1, f_w2, f_b2, f_w3, f_b3, f_freq, f_w4)
    tcol = _tile(n2 * d, 4096, LANES)
    ga = _dft_rows(fwd, g.reshape(n1, n2 * d), tcol)
    spec = _dft_mid(mid, ga.reshape(2, hk, n2, d))
    va = _dft_rows(fwd[:, :hk], vv.reshape(hk, n2 * d), tcol)
    vb = _dft_mid(mid, va.reshape(2, hk, n2, d), spec, mid_t)
    t = _dft_rows_inv(inv, vb.reshape(n1, n2 * d), vv.reshape(hk, n2 * d), x0.reshape(hk, n2 * d), skip, d)
    return _matmul(t.reshape(n, d), [(w_out, 0)], [(d, tn, BF16)], _ep_plain, tm=tm, tn=tn, name="hy_out_proj")[0]


def kernel(x, c, ctx, c_ctx, ada_down, ada_up, ada_bias, norm_gain, ffn_w_gate_up, ffn_w_down, sc_w_in, sc_conv, sc_w_out, mla_w_down, mla_q_norm, mla_kv_norm, mla_w_uq, mla_w_ukv, mla_w_out, hy_w_in, hy_conv, hy_conv_b, hy_f_w1, hy_f_b1, hy_f_w2, hy_f_b2, hy_f_w3, hy_f_b3, hy_f_freq, hy_f_w4, hy_skip, hy_w_out):
    batch, n, d = x.shape
    assert batch == 1 and c.shape[0] == 1 and ctx.shape[0] == 1
    depth = ada_down.shape[0]
    n_mixers = 3
    xs = x.reshape(n, d)
    cs = ctx.reshape(ctx.shape[1], d)

    mla_layers = [i for i in range(depth) if i % n_mixers == 1]
    last_ctx_read = mla_layers[-1] if mla_layers else -1

    s_raw = jnp.concatenate([c.reshape(1, d), c_ctx.reshape(1, d), jnp.zeros((2 * SUBLANES - 2, d), F32)], axis=0)
    mods = _adaln(s_raw, ada_down, ada_up, ada_bias)

    def mod_vecs(i, row):
        return [mods[i, row, m * d:(m + 1) * d] for m in range(N_MOD)]

    ffn_gu, ffn_dn = ffn_w_gate_up.astype(BF16), ffn_w_down.astype(BF16)
    sc_in, sc_out = sc_w_in.astype(BF16), sc_w_out.astype(BF16)
    hy_in, hy_out = hy_w_in.astype(BF16), hy_w_out.astype(BF16)
    mla_out = mla_w_out.astype(BF16)
    pend = None
    pend_c = None
    for i in range(depth):
        kind, j = i % n_mixers, i // n_mixers
        ctx_full = i < last_ctx_read
        ctx_keys = i == last_ctx_read
        g = norm_gain[i]
        streams = [(0, xs, pend)]
        if ctx_full or ctx_keys:
            streams.append((1, cs, pend_c))
        hs = {}
        cur = {}
        for row, xv, pd in streams:
            mv = mod_vecs(i, row)
            if pd is None:
                _, hh = _resid_norm_mod(xv, None, _vec_rows(d, g[0], g[0], g[0], mv[0], mv[1]), has_h=True)
            else:
                xv, hh = _resid_norm_mod(xv, pd[0], _vec_rows(d, pd[1], pd[2], g[0], mv[0], mv[1]), has_h=True)
            hs[row], cur[row] = hh, xv

        ys = {}
        if kind == 0:
            w_in, w_out = (sc_in, j), (sc_out, j)
            ys[0] = _short_conv(hs[0], w_in, sc_conv[j], w_out)
            if ctx_full:
                ys[1] = _short_conv(hs[1], w_in, sc_conv[j], w_out)
        elif kind == 1:
            ys[0] = _mla(hs[0], hs[1], mla_w_down[j], mla_q_norm[j], mla_kv_norm[j], mla_w_uq[j], mla_w_ukv[j],
                         (mla_out, j))
            assert not ctx_full
        else:
            hp = ((hy_in, j), hy_conv[j], hy_conv_b[j], hy_f_w1[j], hy_f_b1[j], hy_f_w2[j], hy_f_b2[j],
                  hy_f_w3[j], hy_f_b3[j], hy_f_freq[j], hy_f_w4[j], hy_skip[j], (hy_out, j))
            ys[0] = _hyena(hs[0], *hp)
            if ctx_full:
                ys[1] = _hyena(hs[1], *hp)

        w_gu, w_dn = (ffn_gu, i), (ffn_dn, i)
        new_pend = {0: None, 1: None}
        for row in ys:
            mv = mod_vecs(i, row)
            xv, h2 = _resid_norm_mod(cur[row], ys[row], _vec_rows(d, mv[2], g[1], g[2], mv[3], mv[4]), has_h=True)
            cur[row] = xv
            new_pend[row] = (_ffn(h2, w_gu, w_dn), mv[5], g[3])
        xs, pend = cur[0], new_pend[0]
        if ctx_full:
            cs, pend_c = cur[1], new_pend[1]
        else:
            pend_c = None

    xs, _ = _resid_norm_mod(xs, pend[0], _vec_rows(d, pend[1], pend[2], pend[2], pend[1], pend[1]), has_h=False)
    return xs.reshape(batch, n, d)
```

```python
import functools
import math

import jax
import jax.numpy as jnp
from jax import lax
from jax.experimental import pallas as pl
from jax.experimental.pallas import tpu as pltpu

F32 = jnp.float32
BF16 = jnp.bfloat16

EPS = 1e-6
N_MOD = 6
NOPE_DIM = 128
ROPE_DIM = 64
V_DIM = 128
GRID_W = 64
ROPE_BASE = 10000.0
HY_BANDS = 16
HY_DECAY_TARGET = 1e-2
HY_FAST_DECAY_PCT = 0.3
HY_SLOW_DECAY_PCT = 1.5

LANES = 128
SUBLANES = 8
VMEM_LIMIT_BYTES = 56 * 1024 * 1024
DFT_INNER = 128
HALO_ROWS = 16
MXU_DEPTH = 256


def _tile(dim, pref, align):
    best = None
    t = align
    while t <= min(dim, pref):
        if dim % t == 0:
            best = t
        t += align
    return best if best is not None else dim


def _params(sem):
    return pltpu.CompilerParams(dimension_semantics=sem, vmem_limit_bytes=VMEM_LIMIT_BYTES)


def _split_hi_lo(x):
    hi = x.astype(BF16)
    lo = (x - hi.astype(F32)).astype(BF16)
    return hi, lo


def _dot(a, b):
    return jnp.dot(a, b, preferred_element_type=F32)


def _dot3(a, b):
    ah, al = _split_hi_lo(a)
    bh, bl = _split_hi_lo(b)
    return _dot(ah, bh) + _dot(ah, bl) + _dot(al, bh)


def _rms(x, gain):
    return x * lax.rsqrt(jnp.mean(x * x, axis=-1, keepdims=True) + EPS) * gain


def _silu(x):
    return x * (1.0 / (1.0 + jnp.exp(-x)))


def _mm_body(*refs, n_w, n_extra, n_out, nk, epilogue):
    a_ref = refs[0]
    w_refs = refs[1:1 + n_w]
    extra_refs = refs[1 + n_w:1 + n_w + n_extra]
    out_refs = refs[1 + n_w + n_extra:1 + n_w + n_extra + n_out]
    acc_refs = refs[1 + n_w + n_extra + n_out:]
    a = a_ref[...].astype(BF16)
    dots = [_dot(a, w[...].astype(BF16)) for w in w_refs]
    if nk == 1:
        epilogue(dots, extra_refs, out_refs)
        return
    k = pl.program_id(2)

    @pl.when(k == 0)
    def _():
        for acc, d in zip(acc_refs, dots):
            acc[...] = d

    @pl.when(k > 0)
    def _():
        for acc, d in zip(acc_refs, dots):
            acc[...] += d

    @pl.when(k == nk - 1)
    def _():
        epilogue([acc[...] for acc in acc_refs], extra_refs, out_refs)


def _matmul(a, ws, outs, epilogue, *, extras=(), tm, tn, tk=None, name):
    M, K = a.shape
    tk = K if tk is None else tk
    nk = K // tk
    n_col_blocks = outs[0][0] // outs[0][1]
    grid = (M // tm, n_col_blocks, nk)
    in_specs = [pl.BlockSpec((tm, tk), lambda i, j, k: (i, k))]
    operands = [a]
    for w, off in ws:
        if isinstance(w, tuple):
            w, layer = w
            in_specs.append(pl.BlockSpec((None, tk, tn),
                                         functools.partial(lambda i, j, k, o, l: (l, k, j + o), o=off, l=layer)))
        else:
            in_specs.append(pl.BlockSpec((tk, tn), functools.partial(lambda i, j, k, o: (k, j + o), o=off)))
        operands.append(w)
    for arr, bshape, imap in extras:
        in_specs.append(pl.BlockSpec(bshape, imap))
        operands.append(arr)
    out_shape = [jax.ShapeDtypeStruct((M, wt), dt) for wt, _, dt in outs]
    out_specs = [pl.BlockSpec((tm, bw), lambda i, j, k: (i, j)) for _, bw, _ in outs]
    scratch = [pltpu.VMEM((tm, tn), F32) for _ in ws] if nk > 1 else []
    body = functools.partial(_mm_body, n_w=len(ws), n_extra=len(extras), n_out=len(outs), nk=nk,
                             epilogue=epilogue)
    res = pl.pallas_call(
        body, grid=grid, in_specs=in_specs, out_specs=out_specs, out_shape=out_shape,
        scratch_shapes=scratch, name=name,
        compiler_params=_params(("parallel", "arbitrary", "arbitrary")),
    )(*operands)
    return res


def _ep_plain(dots, extras, outs):
    outs[0][...] = dots[0].astype(outs[0].dtype)


def _ep_gate_pair(dots, extras, outs):
    outs[0][...] = dots[0].astype(outs[0].dtype)
    outs[1][...] = (dots[1] * dots[2]).astype(outs[1].dtype)


def _ep_swiglu(dots, extras, outs):
    outs[0][...] = (_silu(dots[0]) * dots[1]).astype(outs[0].dtype)


def _ep_rms(dots, extras, outs):
    outs[0][...] = _rms(dots[0], extras[0][...]).astype(outs[0].dtype)


def _ep_kv_down(dots, extras, outs, *, kv_rank):
    gain_ref, tab_ref = extras
    d = dots[0]
    outs[0][...] = _rms(d[:, :kv_rank], gain_ref[...]).astype(outs[0].dtype)
    p = d[:, kv_rank:] * tab_ref[...]
    outs[1][...] = (p + pltpu.roll(p, ROPE_DIM, axis=1)).astype(outs[1].dtype)


def _ep_q_up(dots, extras, outs):
    tab = extras[0][...]
    x = dots[0]
    width = tab.shape[1]
    for g in range(x.shape[1] // width):
        outs[0][:, g * width:(g + 1) * width] = (x[:, g * width:(g + 1) * width] * tab).astype(outs[0].dtype)


def _ep_k_up(dots, extras, outs):
    kn = dots[0]
    kr = extras[0][...]
    for g in range(kn.shape[1] // NOPE_DIM):
        base = g * (NOPE_DIM + LANES)
        outs[0][:, base:base + NOPE_DIM] = kn[:, g * NOPE_DIM:(g + 1) * NOPE_DIM].astype(outs[0].dtype)
        outs[0][:, base + NOPE_DIM:base + NOPE_DIM + LANES] = kr


def _adaln_body(s_ref, down_ref, up_ref, bias_ref, out_ref, t_ref):
    @pl.when(pl.program_id(1) == 0)
    def _():
        t_ref[...] = _dot3(_silu(s_ref[...]), down_ref[0])

    out_ref[0] = _dot3(t_ref[...], up_ref[0]) + bias_ref[0]


def _adaln(s_raw, ada_down, ada_up, ada_bias):
    depth, d, r = ada_down.shape
    n6 = ada_up.shape[2]
    rows = s_raw.shape[0]
    tn = _tile(n6, 2048, LANES)
    return pl.pallas_call(
        _adaln_body, grid=(depth, n6 // tn),
        in_specs=[pl.BlockSpec((rows, d), lambda i, j: (0, 0)),
                  pl.BlockSpec((1, d, r), lambda i, j: (i, 0, 0)),
                  pl.BlockSpec((1, r, tn), lambda i, j: (i, 0, j)),
                  pl.BlockSpec((1, 1, tn), lambda i, j: (i, 0, j))],
        out_specs=pl.BlockSpec((1, rows, tn), lambda i, j: (i, 0, j)),
        out_shape=jax.ShapeDtypeStruct((depth, rows, n6), F32),
        scratch_shapes=[pltpu.VMEM((rows, r), F32)], name="adaln",
        compiler_params=_params(("arbitrary", "arbitrary")),
    )(s_raw, ada_down, ada_up, ada_bias.reshape(depth, 1, n6))


def _rnm_body(*refs, has_resid, has_h):
    refs = list(refs)
    x_ref = refs.pop(0)
    y_ref = refs.pop(0) if has_resid else None
    vec_ref = refs.pop(0)
    x = x_ref[...]
    if has_resid:
        xo_ref = refs.pop(0)
        x = x + vec_ref[0:1, :] * _rms(y_ref[...].astype(F32), vec_ref[1:2, :])
        xo_ref[...] = x
    if has_h:
        h_ref = refs.pop(0)
        h_ref[...] = (_rms(x, vec_ref[2:3, :]) * (1.0 + vec_ref[4:5, :]) + vec_ref[3:4, :]).astype(h_ref.dtype)


def _resid_norm_mod(x, y, vec, *, has_h, rows=None):
    n, d = x.shape
    n = n if rows is None else rows
    has_resid = y is not None
    tr = _tile(n, 256, 16)
    spec = pl.BlockSpec((tr, d), lambda i: (i, 0))
    in_specs = [spec] + ([spec] if has_resid else []) + [pl.BlockSpec(vec.shape, lambda i: (0, 0))]
    operands = [x] + ([y] if has_resid else []) + [vec]
    out_shape, out_specs = [], []
    if has_resid:
        out_shape.append(jax.ShapeDtypeStruct((n, d), F32))
        out_specs.append(spec)
    if has_h:
        out_shape.append(jax.ShapeDtypeStruct((n, d), BF16))
        out_specs.append(spec)
    res = pl.pallas_call(
        functools.partial(_rnm_body, has_resid=has_resid, has_h=has_h), grid=(n // tr,),
        in_specs=in_specs, out_specs=out_specs, out_shape=out_shape, name="resid_norm_mod",
        compiler_params=_params(("parallel",)),
    )(*operands)
    res = list(res)
    x_new = res.pop(0) if has_resid else None
    h = res.pop(0) if has_h else None
    return x_new, h


def _vec_rows(d, *rows):
    out = [r.reshape(1, d).astype(F32) for r in rows]
    out += [jnp.zeros((1, d), F32)] * (SUBLANES - len(out))
    return jnp.concatenate(out, axis=0)


def _conv3_rows(main, prev_blk, next_blk, w0, w1, w2, is_first, is_last):
    main = main.astype(F32)
    tr = main.shape[0]
    rows = lax.broadcasted_iota(jnp.int32, main.shape, 0)
    prev_row = jnp.where(is_first, 0.0, prev_blk[HALO_ROWS - 1:HALO_ROWS, :].astype(F32))
    next_row = jnp.where(is_last, 0.0, next_blk[0:1, :].astype(F32))
    up = jnp.where(rows == 0, prev_row, pltpu.roll(main, 1, axis=0))
    dn = jnp.where(rows == tr - 1, next_row, pltpu.roll(main, tr - 1, axis=0))
    return w0 * up + w1 * main + w2 * dn


def _halo_specs(tr, tc, n_rows, col_off):
    per = tr // HALO_ROWS
    last = n_rows // HALO_ROWS - 1
    return [
        pl.BlockSpec((tr, tc), lambda i, j: (i, j + col_off)),
        pl.BlockSpec((HALO_ROWS, tc), lambda i, j: (jnp.maximum(i * per - 1, 0), j + col_off)),
        pl.BlockSpec((HALO_ROWS, tc), lambda i, j: (jnp.minimum((i + 1) * per, last), j + col_off)),
    ]


def _sc_gate_body(b_ref, u_ref, up_ref, un_ref, w_ref, o_ref):
    i = pl.program_id(0)
    conv = _conv3_rows(u_ref[...], up_ref[...], un_ref[...], w_ref[0:1, :], w_ref[1:2, :], w_ref[2:3, :],
                       i == 0, i == pl.num_programs(0) - 1)
    o_ref[...] = (b_ref[...].astype(F32) * conv).astype(o_ref.dtype)


def _sc_gate(b, u, w_conv):
    n, d = u.shape
    tr = _tile(n, 512, 16)
    tc = _tile(d, 1024, LANES)
    w8 = jnp.concatenate([w_conv.astype(F32), jnp.zeros((SUBLANES - 3, d), F32)], axis=0)
    return pl.pallas_call(
        _sc_gate_body, grid=(n // tr, d // tc),
        in_specs=[pl.BlockSpec((tr, tc), lambda i, j: (i, j))] + _halo_specs(tr, tc, n, 0)
        + [pl.BlockSpec((SUBLANES, tc), lambda i, j: (0, j))],
        out_specs=pl.BlockSpec((tr, tc), lambda i, j: (i, j)),
        out_shape=jax.ShapeDtypeStruct((n, d), BF16), name="sc_conv_gate",
        compiler_params=_params(("parallel", "parallel")),
    )(b, u, u, u, w8)


def _hy_gate_body(*refs):
    z_refs = refs[0:9]
    w_refs = refs[9:12]
    x0_ref, vv_ref = refs[12:14]
    i = pl.program_id(0)
    first, last = i == 0, i == pl.num_programs(0) - 1
    conv = []
    for g in range(3):
        m, p, nx = z_refs[3 * g:3 * g + 3]
        w = w_refs[g]
        conv.append(_conv3_rows(m[...], p[...], nx[...], w[0:1, :], w[1:2, :], w[2:3, :], first, last) + w[3:4, :])
    x0_ref[...] = conv[0].astype(x0_ref.dtype)
    vv_ref[...] = (conv[1] * conv[2]).astype(vv_ref.dtype)


def _hy_gate(z, conv_w, conv_b):
    n, d3 = z.shape
    d = d3 // 3
    tr = _tile(n, 512, 16)
    tc = _tile(d, 512, LANES)
    ncb = d // tc
    w8 = jnp.concatenate([conv_w.astype(F32), conv_b.reshape(1, d3).astype(F32),
                          jnp.zeros((SUBLANES - 4, d3), F32)], axis=0)
    in_specs, operands = [], []
    for g in range(3):
        in_specs += _halo_specs(tr, tc, n, g * ncb)
        operands += [z, z, z]
    for g in range(3):
        in_specs.append(pl.BlockSpec((SUBLANES, tc), functools.partial(lambda i, j, o: (0, j + o), o=g * ncb)))
        operands.append(w8)
    spec = pl.BlockSpec((tr, tc), lambda i, j: (i, j))
    return pl.pallas_call(
        _hy_gate_body, grid=(n // tr, ncb), in_specs=in_specs, out_specs=[spec, spec],
        out_shape=[jax.ShapeDtypeStruct((n, d), BF16)] * 2, name="hy_conv_gate",
        compiler_params=_params(("parallel", "parallel")),
    )(*operands)


def _attn_body(q_ref, k_ref, vt_ref, o_ref, m_ref, l_ref, acc_ref, *, ck):
    ki = pl.program_id(2)

    @pl.when(ki == 0)
    def _():
        m_ref[...] = jnp.full(m_ref.shape, -jnp.inf, F32)
        l_ref[...] = jnp.zeros(l_ref.shape, F32)
        acc_ref[...] = jnp.zeros(acc_ref.shape, F32)

    q = q_ref[...]
    m, l, acc = m_ref[...], l_ref[...], acc_ref[...]
    tk = k_ref.shape[0]
    bounds = [(lo, min(lo + ck, tk)) for lo in range(0, tk, ck)]

    def scores(b):
        return lax.dot_general(k_ref[b[0]:b[1], :], q, (((1,), (1,)), ((), ())), preferred_element_type=F32)

    s_next = scores(bounds[0])
    pending = None
    for c, b in enumerate(bounds):
        s = s_next
        if c + 1 < len(bounds):
            s_next = scores(bounds[c + 1])
        if pending is not None:
            a_prev, p_prev, b_prev = pending
            acc = a_prev * acc + _dot(vt_ref[:, b_prev[0]:b_prev[1]], p_prev)
        m_new = jnp.maximum(m, jnp.max(s, axis=0, keepdims=True))
        alpha = jnp.exp2(m - m_new)
        p = jnp.exp2(s - m_new)
        l = alpha * l + jnp.sum(p, axis=0, keepdims=True)
        pending = (alpha, p.astype(BF16), b)
        m = m_new
    a_prev, p_prev, b_prev = pending
    acc = a_prev * acc + _dot(vt_ref[:, b_prev[0]:b_prev[1]], p_prev)
    m_ref[...], l_ref[...], acc_ref[...] = m, l, acc

    @pl.when(ki == pl.num_programs(2) - 1)
    def _():
        o_ref[...] = (acc * (1.0 / l)).T.astype(o_ref.dtype)


def _attention(q, k, vt, heads):
    n = q.shape[0]
    nk = k.shape[0]
    qw = q.shape[1] // heads
    unit = MXU_DEPTH if nk % MXU_DEPTH == 0 else LANES
    ck = 3 * MXU_DEPTH
    tq = _tile(n, 2048, LANES)
    tk = _tile(nk, 13 * MXU_DEPTH, unit)
    return pl.pallas_call(
        functools.partial(_attn_body, ck=ck), grid=(heads, n // tq, nk // tk),
        in_specs=[pl.BlockSpec((tq, qw), lambda h, i, j: (i, h)),
                  pl.BlockSpec((tk, qw), lambda h, i, j: (j, h)),
                  pl.BlockSpec((V_DIM, tk), lambda h, i, j: (h, j))],
        out_specs=pl.BlockSpec((tq, V_DIM), lambda h, i, j: (i, h)),
        out_shape=jax.ShapeDtypeStruct((n, heads * V_DIM), BF16),
        scratch_shapes=[pltpu.VMEM((1, tq), F32), pltpu.VMEM((1, tq), F32), pltpu.VMEM((V_DIM, tq), F32)],
        name="mla_flash_attention",
        compiler_params=_params(("parallel", "parallel", "arbitrary")),
    )(q, k, vt)


def _vt_body(w_ref, c_ref, o_ref):
    o_ref[...] = lax.dot_general(w_ref[...], c_ref[...], (((1,), (1,)), ((), ())),
                                 preferred_element_type=F32).astype(o_ref.dtype)


def _v_up_transposed(w_vt, ckv):
    hv, r = w_vt.shape
    nk = ckv.shape[0]
    th = _tile(hv, 2 * V_DIM, V_DIM)
    tn = _tile(nk, 13 * MXU_DEPTH, LANES)
    return pl.pallas_call(
        _vt_body, grid=(hv // th, nk // tn),
        in_specs=[pl.BlockSpec((th, r), lambda i, j: (i, 0)), pl.BlockSpec((tn, r), lambda i, j: (j, 0))],
        out_specs=pl.BlockSpec((th, tn), lambda i, j: (i, j)),
        out_shape=jax.ShapeDtypeStruct((hv, nk), BF16), name="mla_v_up_transposed",
        compiler_params=_params(("parallel", "parallel")),
    )(w_vt, ckv)


def _filter_body(z_ref, w1_ref, w2_ref, w3_ref, bf_ref, w4_ref, dl_ref, o_ref, h_ref):
    @pl.when(pl.program_id(1) == 0)
    def _():
        bf = bf_ref[...]
        h = jnp.sin(bf[3:4, :] * (_dot3(z_ref[...], w1_ref[...]) + bf[0:1, :]))
        h = jnp.sin(bf[4:5, :] * (_dot3(h, w2_ref[...]) + bf[1:2, :]))
        h_ref[...] = jnp.sin(bf[5:6, :] * (_dot3(h, w3_ref[...]) + bf[2:3, :]))

    z = z_ref[...]
    t = z[:, 0:1]
    sign = z[:, HY_BANDS * 2 + 1:HY_BANDS * 2 + 2]
    filt = _dot(h_ref[...].astype(BF16), w4_ref[...].astype(BF16))
    o_ref[...] = (sign * filt * jnp.exp(-t * dl_ref[...])).astype(o_ref.dtype)


def _hyena_filter(n, d, f_w1, f_b1, f_w2, f_b2, f_w3, f_b3, f_freq, f_w4):
    fh = f_w1.shape[1]
    emb = f_w1.shape[0]
    r = jnp.arange(2 * n, dtype=jnp.int32)
    p = jnp.minimum(jnp.where(r < n, r, 2 * n - r), n - 1)
    t = jnp.linspace(0.0, 1.0, n, dtype=F32)[p][:, None]
    w = ((2.0 * math.pi / n) * jnp.arange(n, dtype=F32))[p][:, None]
    bands = jnp.linspace(1e-4, HY_BANDS - 1, HY_BANDS, dtype=F32)
    sign = jnp.where(r < n, 1.0, jnp.where(r == n, 0.0, -1.0)).astype(F32)[:, None]
    zw = LANES // 2
    z = jnp.concatenate([t, jnp.cos(bands * w), -jnp.sin(bands * w), sign,
                         jnp.zeros((2 * n, zw - emb - 1), F32)], axis=-1)
    w1p = jnp.concatenate([f_w1.astype(F32), jnp.zeros((zw - emb, fh), F32)], axis=0)
    bf = jnp.concatenate([f_b1.reshape(1, fh), f_b2.reshape(1, fh), f_b3.reshape(1, fh),
                          f_freq.reshape(3, fh), jnp.zeros((2, fh), F32)], axis=0).astype(F32)
    deltas = jnp.abs(jnp.linspace(math.log(HY_DECAY_TARGET) / HY_SLOW_DECAY_PCT,
                                  math.log(HY_DECAY_TARGET) / HY_FAST_DECAY_PCT, d, dtype=F32)).reshape(1, d)
    tr = _tile(n, 512, SUBLANES)
    tc = _tile(d, 1024, LANES)
    ncb = d // tc
    half = n // tr
    const = lambda shape: pl.BlockSpec(shape, lambda i, j: (0, 0))
    return pl.pallas_call(
        _filter_body, grid=(2 * n // tr, ncb),
        in_specs=[pl.BlockSpec((tr, zw), lambda i, j: (i, 0)), const((zw, fh)), const((fh, fh)), const((fh, fh)),
                  const((SUBLANES, fh)),
                  pl.BlockSpec((fh, tc), lambda i, j: (0, j + jnp.where(i >= half, ncb, 0))),
                  pl.BlockSpec((1, tc), lambda i, j: (0, j))],
        out_specs=pl.BlockSpec((tr, tc), lambda i, j: (i, j)),
        out_shape=jax.ShapeDtypeStruct((2 * n, d), BF16),
        scratch_shapes=[pltpu.VMEM((tr, fh), F32)], name="hyena_filter",
        compiler_params=_params(("parallel", "arbitrary")),
    )(z, w1p, f_w2.astype(F32), f_w3.astype(F32), bf, f_w4.astype(F32), deltas)


def _dft_tables(n):
    big_n = 2 * n
    n2 = DFT_INNER
    n1 = big_n // n2
    hk = n1 // 2
    k1 = jnp.arange(hk, dtype=jnp.int32)
    m1 = jnp.arange(n1, dtype=jnp.int32)
    ang_a = (math.pi / n1) * ((m1[None, :] * (2 * k1[:, None] + 1)) % (2 * n1)).astype(F32)
    fwd = jnp.concatenate([jnp.cos(ang_a), -jnp.sin(ang_a)], axis=0)
    inv = (2.0 / big_n) * jnp.concatenate([jnp.cos(ang_a[:, :hk]).T, -jnp.sin(ang_a[:, :hk]).T], axis=1)
    k2 = jnp.arange(n2, dtype=jnp.int32)
    m2 = jnp.arange(n2, dtype=jnp.int32)
    freq = 2 * k1[:, None, None] + 1 + 2 * n1 * k2[None, :, None]
    ang_c = (math.pi / big_n) * ((m2[None, None, :] * freq) % (2 * big_n)).astype(F32)
    gr, gi = jnp.cos(ang_c), -jnp.sin(ang_c)
    mid = jnp.concatenate([jnp.concatenate([gr, -gi], axis=2), jnp.concatenate([gi, gr], axis=2)], axis=1)
    mid_t = jnp.swapaxes(mid, 1, 2)
    return fwd, inv, mid, mid_t


def _dft_rows_body(f_ref, x_ref, o_ref):
    o_ref[...] = _dot(f_ref[...], x_ref[...]).astype(o_ref.dtype)


def _dft_rows(f, x2d, tn):
    r, k = f.shape
    c = x2d.shape[1]
    return pl.pallas_call(
        _dft_rows_body, grid=(c // tn,),
        in_specs=[pl.BlockSpec((r, k), lambda j: (0, 0)), pl.BlockSpec((k, tn), lambda j: (0, j))],
        out_specs=pl.BlockSpec((r, tn), lambda j: (0, j)),
        out_shape=jax.ShapeDtypeStruct((r, c), BF16), name="hyena_dft_outer",
        compiler_params=_params(("parallel",)),
    )(f.astype(BF16), x2d)


def _dft_rows_inv_body(f_ref, b_ref, v_ref, x0_ref, skip_ref, o_ref):
    y = _dot(f_ref[...], b_ref[...])
    v = v_ref[...].astype(F32)
    o_ref[...] = (x0_ref[...].astype(F32) * (y + v * skip_ref[...])).astype(o_ref.dtype)


def _dft_rows_inv(f, b2d, v2d, x02d, skip, d):
    r, k = f.shape
    c = b2d.shape[1]
    col = pl.BlockSpec((r, d), lambda j: (0, j))
    return pl.pallas_call(
        _dft_rows_inv_body, grid=(c // d,),
        in_specs=[pl.BlockSpec((r, k), lambda j: (0, 0)), pl.BlockSpec((k, d), lambda j: (0, j)), col, col,
                  pl.BlockSpec((1, d), lambda j: (0, 0))],
        out_specs=col, out_shape=jax.ShapeDtypeStruct((r, c), BF16), name="hyena_dft_outer_inverse",
        compiler_params=_params(("parallel",)),
    )(f.astype(BF16), b2d, v2d, x02d, skip.reshape(1, d).astype(F32))


def _stack_re_im(ref):
    _, _, n2, tc = ref.shape
    return ref[:, 0].reshape(2 * n2, tc)


def _dft_mid_body(m_ref, a_ref, o_ref):
    n2 = a_ref.shape[2]
    s = _dot(m_ref[0], _stack_re_im(a_ref))
    o_ref[0, 0] = s[:n2].astype(o_ref.dtype)
    o_ref[1, 0] = s[n2:].astype(o_ref.dtype)


def _dft_conv_body(m_ref, t_ref, a_ref, g_ref, o_ref):
    n2 = a_ref.shape[2]
    s = _dot(m_ref[0], _stack_re_im(a_ref))
    sr, si = s[:n2], s[n2:]
    gr, gi = g_ref[0, 0].astype(F32), g_ref[1, 0].astype(F32)
    y = jnp.concatenate([sr * gr - si * gi, sr * gi + si * gr], axis=0).astype(BF16)
    b = _dot(t_ref[0], y)
    o_ref[0, 0] = b[:n2].astype(o_ref.dtype)
    o_ref[1, 0] = b[n2:].astype(o_ref.dtype)


def _dft_mid(mid, a4, spec4=None, mid_t=None):
    _, hk, n2, d = a4.shape
    tc = _tile(d, 1024, LANES)
    mspec = pl.BlockSpec((1, 2 * n2, 2 * n2), lambda k, j: (k, 0, 0))
    dspec = pl.BlockSpec((2, 1, n2, tc), lambda k, j: (0, k, 0, j))
    if spec4 is None:
        body, in_specs, operands, name = _dft_mid_body, [mspec, dspec], [mid.astype(BF16), a4], "hyena_dft_inner"
    else:
        body, in_specs = _dft_conv_body, [mspec, mspec, dspec, dspec]
        operands, name = [mid.astype(BF16), mid_t.astype(BF16), a4, spec4], "hyena_dft_inner_conv"
    return pl.pallas_call(
        body, grid=(hk, d // tc), in_specs=in_specs, out_specs=dspec,
        out_shape=jax.ShapeDtypeStruct(a4.shape, BF16), name=name,
        compiler_params=_params(("parallel", "arbitrary")),
    )(*operands)


def _row_tile(m):
    return _tile(m, 1024, 16)


def _short_conv(h, w_in, w_conv, w_out):
    m, d = h.shape
    tm = _row_tile(m)
    tn_in = _tile(d, 256, LANES)
    ncb = d // tn_in
    b, u = _matmul(h, [(w_in, 0), (w_in, ncb), (w_in, 2 * ncb)], [(d, tn_in, BF16), (d, tn_in, BF16)],
                   _ep_gate_pair, tm=tm, tn=tn_in, name="sc_in_proj")
    t = _sc_gate(b, u, w_conv)
    tn = _tile(d, 512, LANES)
    return _matmul(t, [(w_out, 0)], [(d, tn, BF16)], _ep_plain, tm=tm, tn=tn, name="sc_out_proj")[0]


def _ffn(h2, w_gate_up, w_down):
    m, d = h2.shape
    f = w_down[0].shape[1]
    tm = _row_tile(m)
    tf = _tile(f, 512, LANES)
    a = _matmul(h2, [(w_gate_up, 0), (w_gate_up, f // tf)], [(f, tf, BF16)], _ep_swiglu,
                tm=tm, tn=tf, name="ffn_gate_up")[0]
    tk = f if f <= 4096 else _tile(f, 6144, LANES)
    tn = _tile(d, 512, LANES)
    return _matmul(a, [(w_down, 0)], [(d, tn, BF16)], _ep_plain, tm=tm, tn=tn, tk=tk, name="ffn_down")[0]


def _rope_tables(n, n_ctx, scale):
    rows = n // GRID_W
    row = jnp.repeat(jnp.arange(rows, dtype=F32), GRID_W)
    col = jnp.tile(jnp.arange(GRID_W, dtype=F32), rows)
    axis_dim = ROPE_DIM // 2
    inv = ROPE_BASE ** (-jnp.arange(0, axis_dim, 2, dtype=F32) / axis_dim)
    ang_r, ang_c = row[:, None] * inv, col[:, None] * inv
    cos = jnp.concatenate([jnp.cos(ang_r), jnp.cos(ang_c)], axis=1)
    sin = jnp.concatenate([jnp.sin(ang_r), jnp.sin(ang_c)], axis=1)
    rot = jnp.concatenate([cos, cos, -sin, sin], axis=1)
    tab_q = scale * jnp.concatenate([jnp.ones((n, NOPE_DIM), F32), rot], axis=1)
    ctx_rot = jnp.concatenate([jnp.ones((n_ctx, ROPE_DIM), F32), jnp.zeros((n_ctx, ROPE_DIM), F32)], axis=1)
    return tab_q, rot, ctx_rot


def _mla(h, hc, w_down, q_norm, kv_norm, w_uq, w_ukv, w_out):
    n, d = h.shape
    n_ctx = hc.shape[0]
    q_rank, kv_rank = w_uq.shape[0], w_ukv.shape[0]
    heads = w_uq.shape[1] // (NOPE_DIM + ROPE_DIM)
    scale = (NOPE_DIM + ROPE_DIM) ** -0.5 * math.log2(math.e)
    tab_q, tab_k, tab_kc = _rope_tables(n, n_ctx, scale)

    half = ROPE_DIM // 4
    idx_a = jnp.concatenate([jnp.arange(0, half), jnp.arange(2 * half, 3 * half)])
    idx_b = jnp.concatenate([jnp.arange(half, 2 * half), jnp.arange(3 * half, 4 * half)])
    w_r = w_down[:, q_rank + kv_rank:]
    w_dq = w_down[:, :q_rank].astype(BF16)
    w_dkv = jnp.concatenate([w_down[:, q_rank:q_rank + kv_rank], w_r[:, idx_a], w_r[:, idx_b],
                             w_r[:, idx_b], w_r[:, idx_a]], axis=1).astype(BF16)
    wq3 = w_uq.reshape(q_rank, heads, NOPE_DIM + ROPE_DIM)
    wq_r = wq3[:, :, NOPE_DIM:]
    w_uq_p = jnp.concatenate([wq3[:, :, :NOPE_DIM], wq_r[:, :, idx_a], wq_r[:, :, idx_b],
                              wq_r[:, :, idx_b], wq_r[:, :, idx_a]], axis=2)
    qw = NOPE_DIM + 2 * ROPE_DIM
    w_uq_p = w_uq_p.reshape(q_rank, heads * qw).astype(BF16)
    wkv3 = w_ukv.reshape(kv_rank, heads, NOPE_DIM + V_DIM)
    w_kn = wkv3[:, :, :NOPE_DIM].reshape(kv_rank, heads * NOPE_DIM).astype(BF16)
    w_vt = wkv3[:, :, NOPE_DIM:].reshape(kv_rank, heads * V_DIM).T.astype(BF16)

    tm = _row_tile(n)
    cqn = _matmul(h, [(w_dq, 0)], [(q_rank, q_rank, BF16)], _ep_rms,
                  extras=[(q_norm.reshape(1, q_rank).astype(F32), (1, q_rank), lambda i, j, k: (0, 0))],
                  tm=tm, tn=q_rank, name="mla_q_down")[0]

    def kv_down(hh, tab, name):
        m = hh.shape[0]
        tmk = _row_tile(m)
        wd = kv_rank + 2 * ROPE_DIM
        return _matmul(hh, [(w_dkv, 0)], [(kv_rank, kv_rank, BF16), (LANES, LANES, BF16)],
                       functools.partial(_ep_kv_down, kv_rank=kv_rank),
                       extras=[(kv_norm.reshape(1, kv_rank).astype(F32), (1, kv_rank), lambda i, j, k: (0, 0)),
                               (tab, (tmk, 2 * ROPE_DIM), lambda i, j, k: (i, 0))],
                       tm=tmk, tn=wd, name=name)

    ckv, kr = kv_down(h, tab_k, "mla_kv_down")
    ckv_c, kr_c = kv_down(hc, tab_kc, "mla_kv_down_ctx")
    ckv = jnp.concatenate([ckv, ckv_c], axis=0)
    kr = jnp.concatenate([kr, kr_c], axis=0)
    nk = n + n_ctx

    gq = 2 if heads % 2 == 0 else 1
    q = _matmul(cqn, [(w_uq_p, 0)], [(heads * qw, gq * qw, BF16)], _ep_q_up,
                extras=[(tab_q, (tm, qw), lambda i, j, k: (i, 0))], tm=tm, tn=gq * qw, name="mla_q_up")[0]
    tmk = _tile(nk, 1664, 16)
    k = _matmul(ckv, [(w_kn, 0)], [(heads * qw, gq * qw, BF16)], _ep_k_up,
                extras=[(kr, (tmk, LANES), lambda i, j, k: (i, 0))], tm=tmk, tn=gq * NOPE_DIM, name="mla_k_up")[0]
    vt = _v_up_transposed(w_vt, ckv)
    o = _attention(q, k, vt, heads)
    tn = _tile(d, 512, LANES)
    return _matmul(o, [(w_out, 0)], [(d, tn, BF16)], _ep_plain, tm=tm, tn=tn, name="mla_out_proj")[0]


def _hyena(h, w_in, conv_w, conv_b, f_w1, f_b1, f_w2, f_b2, f_w3, f_b3, f_freq, f_w4, skip, w_out):
    n, d = h.shape
    tm = _row_tile(n)
    tn = _tile(d, 512, LANES)
    z = _matmul(h, [(w_in, 0)], [(3 * d, tn, BF16)], _ep_plain, tm=tm, tn=tn, name="hy_in_proj")[0]
    x0, vv = _hy_gate(z, conv_w, conv_b)
    n2 = DFT_INNER
    n1 = 2 * n // n2
    hk = n1 // 2
    fwd, inv, mid, mid_t = _dft_tables(n)
    g = _hyena_filter(n, d, f_w1, f_b1, f_w2, f_b2, f_w3, f_b3, f_freq, f_w4)
    tcol = _tile(n2 * d, 4096, LANES)
    ga = _dft_rows(fwd, g.reshape(n1, n2 * d), tcol)
    spec = _dft_mid(mid, ga.reshape(2, hk, n2, d))
    va = _dft_rows(fwd[:, :hk], vv.reshape(hk, n2 * d), tcol)
    vb = _dft_mid(mid, va.reshape(2, hk, n2, d), spec, mid_t)
    t = _dft_rows_inv(inv, vb.reshape(n1, n2 * d), vv.reshape(hk, n2 * d), x0.reshape(hk, n2 * d), skip, d)
    return _matmul(t.reshape(n, d), [(w_out, 0)], [(d, tn, BF16)], _ep_plain, tm=tm, tn=tn, name="hy_out_proj")[0]


def kernel(x, c, ctx, c_ctx, ada_down, ada_up, ada_bias, norm_gain, ffn_w_gate_up, ffn_w_down, sc_w_in, sc_conv, sc_w_out, mla_w_down, mla_q_norm, mla_kv_norm, mla_w_uq, mla_w_ukv, mla_w_out, hy_w_in, hy_conv, hy_conv_b, hy_f_w1, hy_f_b1, hy_f_w2, hy_f_b2, hy_f_w3, hy_f_b3, hy_f_freq, hy_f_w4, hy_skip, hy_w_out):
    batch, n, d = x.shape
    assert batch == 1 and c.shape[0] == 1 and ctx.shape[0] == 1
    depth = ada_down.shape[0]
    n_mixers = 3
    xs = x.reshape(n, d)
    cs = ctx.reshape(ctx.shape[1], d)

    mla_layers = [i for i in range(depth) if i % n_mixers == 1]
    last_ctx_read = mla_layers[-1] if mla_layers else -1

    s_raw = jnp.concatenate([c.reshape(1, d), c_ctx.reshape(1, d), jnp.zeros((2 * SUBLANES - 2, d), F32)], axis=0)
    mods = _adaln(s_raw, ada_down, ada_up, ada_bias)

    def mod_vecs(i, row):
        return [mods[i, row, m * d:(m + 1) * d] for m in range(N_MOD)]

    ffn_gu, ffn_dn = ffn_w_gate_up.astype(BF16), ffn_w_down.astype(BF16)
    sc_in, sc_out = sc_w_in.astype(BF16), sc_w_out.astype(BF16)
    hy_in, hy_out = hy_w_in.astype(BF16), hy_w_out.astype(BF16)
    mla_out = mla_w_out.astype(BF16)
    pend = None
    pend_c = None
    for i in range(depth):
        kind, j = i % n_mixers, i // n_mixers
        ctx_full = i < last_ctx_read
        ctx_keys = i == last_ctx_read
        g = norm_gain[i]
        streams = [(0, xs, pend)]
        if ctx_full or ctx_keys:
            streams.append((1, cs, pend_c))
        hs = {}
        cur = {}
        for row, xv, pd in streams:
            mv = mod_vecs(i, row)
            if pd is None:
                _, hh = _resid_norm_mod(xv, None, _vec_rows(d, g[0], g[0], g[0], mv[0], mv[1]), has_h=True)
            else:
                xv, hh = _resid_norm_mod(xv, pd[0], _vec_rows(d, pd[1], pd[2], g[0], mv[0], mv[1]), has_h=True)
            hs[row], cur[row] = hh, xv

        ys = {}
        if kind == 0:
            w_in, w_out = (sc_in, j), (sc_out, j)
            ys[0] = _short_conv(hs[0], w_in, sc_conv[j], w_out)
            if ctx_full:
                ys[1] = _short_conv(hs[1], w_in, sc_conv[j], w_out)
        elif kind == 1:
            ys[0] = _mla(hs[0], hs[1], mla_w_down[j], mla_q_norm[j], mla_kv_norm[j], mla_w_uq[j], mla_w_ukv[j],
                         (mla_out, j))
            assert not ctx_full
        else:
            hp = ((hy_in, j), hy_conv[j], hy_conv_b[j], hy_f_w1[j], hy_f_b1[j], hy_f_w2[j], hy_f_b2[j],
                  hy_f_w3[j], hy_f_b3[j], hy_f_freq[j], hy_f_w4[j], hy_skip[j], (hy_out, j))
            ys[0] = _hyena(hs[0], *hp)
            if ctx_full:
                ys[1] = _hyena(hs[1], *hp)

        w_gu, w_dn = (ffn_gu, i), (ffn_dn, i)
        new_pend = {0: None, 1: None}
        for row in ys:
            mv = mod_vecs(i, row)
            xv, h2 = _resid_norm_mod(cur[row], ys[row], _vec_rows(d, mv[2], g[1], g[2], mv[3], mv[4]), has_h=True)
            cur[row] = xv
            new_pend[row] = (_ffn(h2, w_gu, w_dn), mv[5], g[3])
        xs, pend = cur[0], new_pend[0]
        if ctx_full:
            cs, pend_c = cur[1], new_pend[1]
        else:
            pend_c = None

    xs, _ = _resid_norm_mod(xs, pend[0], _vec_rows(d, pend[1], pend[2], pend[2], pend[1], pend[1]), has_h=False)
    return xs.reshape(batch, n, d)
```

```python
import functools
import math

import jax
import jax.numpy as jnp
from jax import lax
from jax.experimental import pallas as pl
from jax.experimental.pallas import tpu as pltpu

F32 = jnp.float32
BF16 = jnp.bfloat16

EPS = 1e-6
N_MOD = 6
NOPE_DIM = 128
ROPE_DIM = 64
V_DIM = 128
GRID_W = 64
ROPE_BASE = 10000.0
HY_BANDS = 16
HY_DECAY_TARGET = 1e-2
HY_FAST_DECAY_PCT = 0.3
HY_SLOW_DECAY_PCT = 1.5

LANES = 128
SUBLANES = 8
VMEM_LIMIT_BYTES = 56 * 1024 * 1024
DFT_INNER = 128
HALO_ROWS = 16
MXU_DEPTH = 256

def _tile(dim, pref, align):
    best = None
    t = align
    while t <= min(dim, pref):
        if dim % t == 0:
            best = t
        t += align
    return best if best is not None else dim


def _params(sem):
    return pltpu.CompilerParams(dimension_semantics=sem, vmem_limit_bytes=VMEM_LIMIT_BYTES)


def _split_hi_lo(x):
    hi = x.astype(BF16)
    lo = (x - hi.astype(F32)).astype(BF16)
    return hi, lo


def _dot(a, b):
    return jnp.dot(a, b, preferred_element_type=F32)


def _dot3(a, b):
    ah, al = _split_hi_lo(a)
    bh, bl = _split_hi_lo(b)
    return _dot(ah, bh) + _dot(ah, bl) + _dot(al, bh)


def _rms(x, gain):
    return x * lax.rsqrt(jnp.mean(x * x, axis=-1, keepdims=True) + EPS) * gain


def _silu(x):
    return x * (1.0 / (1.0 + jnp.exp(-x)))


def _mm_body(*refs, n_lhs, n_w, n_extra, n_out, nk, epilogue, lhs_fn):
    refs = list(refs)
    lhs_refs = [refs.pop(0) for _ in range(n_lhs)]
    w_refs = [refs.pop(0) for _ in range(n_w)]
    extra_refs = [refs.pop(0) for _ in range(n_extra)]
    out_refs = [refs.pop(0) for _ in range(n_out)]
    acc_refs = [refs.pop(0) for _ in range(n_w if nk > 1 else 0)]
    if lhs_fn is None:
        a = lhs_refs[0][...].astype(BF16)
    else:
        lhs_scratch = refs.pop(0)

        @pl.when(pl.program_id(1) == 0)
        def _():
            lhs_fn(lhs_refs, lhs_scratch)

        a = lhs_scratch[...]
    dots = [_dot(a, w[...].astype(BF16)) for w in w_refs]
    if nk == 1:
        epilogue(dots, extra_refs, out_refs)
        return
    k = pl.program_id(2)

    @pl.when(k == 0)
    def _():
        for acc, d in zip(acc_refs, dots):
            acc[...] = d

    @pl.when(k > 0)
    def _():
        for acc, d in zip(acc_refs, dots):
            acc[...] += d

    @pl.when(k == nk - 1)
    def _():
        epilogue([acc[...] for acc in acc_refs], extra_refs, out_refs)


def _matmul(a, ws, outs, epilogue, *, extras=(), tm, tn, tk=None, name, lhs_fn=None, lhs_shape=None):
    if lhs_fn is None:
        M, K = a.shape
    else:
        M, K = lhs_shape
    tk = K if tk is None else tk
    nk = K // tk
    assert lhs_fn is None or nk == 1
    n_col_blocks = outs[0][0] // outs[0][1]
    grid = (M // tm, n_col_blocks, nk)
    if lhs_fn is None:
        in_specs = [pl.BlockSpec((tm, tk), lambda i, j, k: (i, k))]
        operands = [a]
    else:
        in_specs = [pl.BlockSpec(bshape, imap) for _, bshape, imap in a]
        operands = [arr for arr, _, _ in a]
    n_lhs = len(operands)
    for w, off in ws:
        if isinstance(w, tuple):
            w, layer = w
            in_specs.append(pl.BlockSpec((None, tk, tn),
                                         functools.partial(lambda i, j, k, o, l: (l, k, j + o), o=off, l=layer)))
        else:
            in_specs.append(pl.BlockSpec((tk, tn), functools.partial(lambda i, j, k, o: (k, j + o), o=off)))
        operands.append(w)
    for arr, bshape, imap in extras:
        in_specs.append(pl.BlockSpec(bshape, imap))
        operands.append(arr)
    out_shape = [jax.ShapeDtypeStruct((M, wt), dt) for wt, _, dt in outs]
    out_specs = [pl.BlockSpec((tm, bw), lambda i, j, k: (i, j)) for _, bw, _ in outs]
    scratch = [pltpu.VMEM((tm, tn), F32) for _ in ws] if nk > 1 else []
    if lhs_fn is not None:
        scratch.append(pltpu.VMEM((tm, K), BF16))
    body = functools.partial(_mm_body, n_lhs=n_lhs, n_w=len(ws), n_extra=len(extras), n_out=len(outs), nk=nk,
                             epilogue=epilogue, lhs_fn=lhs_fn)
    res = pl.pallas_call(
        body, grid=grid, in_specs=in_specs, out_specs=out_specs, out_shape=out_shape,
        scratch_shapes=scratch, name=name,
        compiler_params=_params(("parallel", "arbitrary", "arbitrary")),
    )(*operands)
    return res


def _ep_plain(dots, extras, outs):
    outs[0][...] = dots[0].astype(outs[0].dtype)


def _ep_gate_pair(dots, extras, outs):
    outs[0][...] = dots[0].astype(outs[0].dtype)
    outs[1][...] = (dots[1] * dots[2]).astype(outs[1].dtype)


def _ep_swiglu(dots, extras, outs):
    outs[0][...] = (_silu(dots[0]) * dots[1]).astype(outs[0].dtype)


def _ep_rms(dots, extras, outs):
    outs[0][...] = _rms(dots[0], extras[0][...]).astype(outs[0].dtype)


def _ep_kv_down(dots, extras, outs, *, kv_rank):
    gain_ref, tab_ref = extras
    d = dots[0]
    outs[0][...] = _rms(d[:, :kv_rank], gain_ref[...]).astype(outs[0].dtype)
    p = d[:, kv_rank:] * tab_ref[...]
    outs[1][...] = (p + pltpu.roll(p, ROPE_DIM, axis=1)).astype(outs[1].dtype)


def _ep_q_up(dots, extras, outs):
    tab = extras[0][...]
    x = dots[0]
    width = tab.shape[1]
    for g in range(x.shape[1] // width):
        outs[0][:, g * width:(g + 1) * width] = (x[:, g * width:(g + 1) * width] * tab).astype(outs[0].dtype)


def _ep_k_up(dots, extras, outs):
    kn = dots[0]
    kr = extras[0][...]
    for g in range(kn.shape[1] // NOPE_DIM):
        base = g * (NOPE_DIM + LANES)
        outs[0][:, base:base + NOPE_DIM] = kn[:, g * NOPE_DIM:(g + 1) * NOPE_DIM].astype(outs[0].dtype)
        outs[0][:, base + NOPE_DIM:base + NOPE_DIM + LANES] = kr


def _adaln_body(s_ref, down_ref, up_ref, bias_ref, out_ref, t_ref):
    @pl.when(pl.program_id(1) == 0)
    def _():
        t_ref[...] = _dot3(_silu(s_ref[...]), down_ref[0])

    out_ref[0] = _dot3(t_ref[...], up_ref[0]) + bias_ref[0]


def _adaln(s_raw, ada_down, ada_up, ada_bias):
    depth, d, r = ada_down.shape
    n6 = ada_up.shape[2]
    rows = s_raw.shape[0]
    tn = _tile(n6, 2048, LANES)
    return pl.pallas_call(
        _adaln_body, grid=(depth, n6 // tn),
        in_specs=[pl.BlockSpec((rows, d), lambda i, j: (0, 0)),
                  pl.BlockSpec((1, d, r), lambda i, j: (i, 0, 0)),
                  pl.BlockSpec((1, r, tn), lambda i, j: (i, 0, j)),
                  pl.BlockSpec((1, 1, tn), lambda i, j: (i, 0, j))],
        out_specs=pl.BlockSpec((1, rows, tn), lambda i, j: (i, 0, j)),
        out_shape=jax.ShapeDtypeStruct((depth, rows, n6), F32),
        scratch_shapes=[pltpu.VMEM((rows, r), F32)], name="adaln",
        compiler_params=_params(("arbitrary", "arbitrary")),
    )(s_raw, ada_down, ada_up, ada_bias.reshape(depth, 1, n6))


def _rnm_body(*refs, has_resid, has_h):
    refs = list(refs)
    x_ref = refs.pop(0)
    y_ref = refs.pop(0) if has_resid else None
    vec_ref = refs.pop(0)
    x = x_ref[...]
    if has_resid:
        xo_ref = refs.pop(0)
        x = x + vec_ref[0:1, :] * _rms(y_ref[...].astype(F32), vec_ref[1:2, :])
        xo_ref[...] = x
    if has_h:
        h_ref = refs.pop(0)
        h_ref[...] = (_rms(x, vec_ref[2:3, :]) * (1.0 + vec_ref[4:5, :]) + vec_ref[3:4, :]).astype(h_ref.dtype)


def _resid_norm_mod(x, y, vec, *, has_h, rows=None):
    n, d = x.shape
    n = n if rows is None else rows
    has_resid = y is not None
    tr = _tile(n, 256, 16)
    spec = pl.BlockSpec((tr, d), lambda i: (i, 0))
    in_specs = [spec] + ([spec] if has_resid else []) + [pl.BlockSpec(vec.shape, lambda i: (0, 0))]
    operands = [x] + ([y] if has_resid else []) + [vec]
    out_shape, out_specs = [], []
    if has_resid:
        out_shape.append(jax.ShapeDtypeStruct((n, d), F32))
        out_specs.append(spec)
    if has_h:
        out_shape.append(jax.ShapeDtypeStruct((n, d), BF16))
        out_specs.append(spec)
    res = pl.pallas_call(
        functools.partial(_rnm_body, has_resid=has_resid, has_h=has_h), grid=(n // tr,),
        in_specs=in_specs, out_specs=out_specs, out_shape=out_shape, name="resid_norm_mod",
        compiler_params=_params(("parallel",)),
    )(*operands)
    res = list(res)
    x_new = res.pop(0) if has_resid else None
    h = res.pop(0) if has_h else None
    return x_new, h


def _vec_rows(d, *rows):
    out = [r.reshape(1, d).astype(F32) for r in rows]
    out += [jnp.zeros((1, d), F32)] * (SUBLANES - len(out))
    return jnp.concatenate(out, axis=0)


def _conv3_rows(main, prev_blk, next_blk, w0, w1, w2, is_first, is_last):
    main = main.astype(F32)
    tr = main.shape[0]
    rows = lax.broadcasted_iota(jnp.int32, main.shape, 0)
    prev_row = jnp.where(is_first, 0.0, prev_blk[HALO_ROWS - 1:HALO_ROWS, :].astype(F32))
    next_row = jnp.where(is_last, 0.0, next_blk[0:1, :].astype(F32))
    up = jnp.where(rows == 0, prev_row, pltpu.roll(main, 1, axis=0))
    dn = jnp.where(rows == tr - 1, next_row, pltpu.roll(main, tr - 1, axis=0))
    return w0 * up + w1 * main + w2 * dn


def _halo_specs(tr, tc, n_rows, col_off):
    per = tr // HALO_ROWS
    last = n_rows // HALO_ROWS - 1
    return [
        pl.BlockSpec((tr, tc), lambda i, j: (i, j + col_off)),
        pl.BlockSpec((HALO_ROWS, tc), lambda i, j: (jnp.maximum(i * per - 1, 0), j + col_off)),
        pl.BlockSpec((HALO_ROWS, tc), lambda i, j: (jnp.minimum((i + 1) * per, last), j + col_off)),
    ]


LHS_CHUNK = 512


def _lhs_sc_gate(refs, scratch):
    b_ref, u_ref, up_ref, un_ref, w_ref = refs
    i = pl.program_id(0)
    first, last = i == 0, i == pl.num_programs(0) - 1
    width = scratch.shape[1]
    cw = min(width, LHS_CHUNK)
    for c in range(0, width, cw):
        sl = slice(c, c + cw)
        conv = _conv3_rows(u_ref[:, sl], up_ref[:, sl], un_ref[:, sl], w_ref[0:1, sl], w_ref[1:2, sl],
                           w_ref[2:3, sl], first, last)
        scratch[:, sl] = (b_ref[:, sl].astype(F32) * conv).astype(scratch.dtype)


def _lhs_hy_gate(refs, scratch):
    x0_ref, y_ref, v_ref, skip_ref = refs
    width = scratch.shape[1]
    cw = min(width, LHS_CHUNK)
    for c in range(0, width, cw):
        sl = slice(c, c + cw)
        yv = y_ref[:, sl].astype(F32) + v_ref[:, sl].astype(F32) * skip_ref[:, sl]
        scratch[:, sl] = (x0_ref[:, sl].astype(F32) * yv).astype(scratch.dtype)


def _row_halo_specs(tm, d, n_rows):
    per = tm // HALO_ROWS
    last = n_rows // HALO_ROWS - 1
    return [
        ((tm, d), lambda i, j, k: (i, 0)),
        ((HALO_ROWS, d), lambda i, j, k: (jnp.maximum(i * per - 1, 0), 0)),
        ((HALO_ROWS, d), lambda i, j, k: (jnp.minimum((i + 1) * per, last), 0)),
    ]


def _hy_gate_body(*refs):
    z_refs = refs[0:9]
    w_refs = refs[9:12]
    x0_ref, vv_ref = refs[12:14]
    i = pl.program_id(0)
    first, last = i == 0, i == pl.num_programs(0) - 1
    conv = []
    for g in range(3):
        m, p, nx = z_refs[3 * g:3 * g + 3]
        w = w_refs[g]
        conv.append(_conv3_rows(m[...], p[...], nx[...], w[0:1, :], w[1:2, :], w[2:3, :], first, last) + w[3:4, :])
    x0_ref[...] = conv[0].astype(x0_ref.dtype)
    vv_ref[...] = (conv[1] * conv[2]).astype(vv_ref.dtype)


def _hy_gate(z, conv_w, conv_b):
    n, d3 = z.shape
    d = d3 // 3
    tr = _tile(n, 512, 16)
    tc = _tile(d, 512, LANES)
    ncb = d // tc
    w8 = jnp.concatenate([conv_w.astype(F32), conv_b.reshape(1, d3).astype(F32),
                          jnp.zeros((SUBLANES - 4, d3), F32)], axis=0)
    in_specs, operands = [], []
    for g in range(3):
        in_specs += _halo_specs(tr, tc, n, g * ncb)
        operands += [z, z, z]
    for g in range(3):
        in_specs.append(pl.BlockSpec((SUBLANES, tc), functools.partial(lambda i, j, o: (0, j + o), o=g * ncb)))
        operands.append(w8)
    spec = pl.BlockSpec((tr, tc), lambda i, j: (i, j))
    return pl.pallas_call(
        _hy_gate_body, grid=(n // tr, ncb), in_specs=in_specs, out_specs=[spec, spec],
        out_shape=[jax.ShapeDtypeStruct((n, d), BF16)] * 2, name="hy_conv_gate",
        compiler_params=_params(("parallel", "parallel")),
    )(*operands)


def _attn_body(q_ref, k_ref, vt_ref, o_ref, m_ref, l_ref, acc_ref, *, ck):
    ki = pl.program_id(2)

    @pl.when(ki == 0)
    def _():
        m_ref[...] = jnp.full(m_ref.shape, -jnp.inf, F32)
        l_ref[...] = jnp.zeros(l_ref.shape, F32)
        acc_ref[...] = jnp.zeros(acc_ref.shape, F32)

    q = q_ref[...]
    m, l, acc = m_ref[...], l_ref[...], acc_ref[...]
    tk = k_ref.shape[0]
    bounds = [(lo, min(lo + ck, tk)) for lo in range(0, tk, ck)]

    def scores(b):
        return lax.dot_general(k_ref[b[0]:b[1], :], q, (((1,), (1,)), ((), ())), preferred_element_type=F32)

    s_next = scores(bounds[0])
    pending = None
    for c, b in enumerate(bounds):
        s = s_next
        if c + 1 < len(bounds):
            s_next = scores(bounds[c + 1])
        if pending is not None:
            a_prev, p_prev, b_prev = pending
            acc = a_prev * acc + _dot(vt_ref[:, b_prev[0]:b_prev[1]], p_prev)
        m_new = jnp.maximum(m, jnp.max(s, axis=0, keepdims=True))
        alpha = jnp.exp2(m - m_new)
        p = jnp.exp2(s - m_new)
        l = alpha * l + jnp.sum(p, axis=0, keepdims=True)
        pending = (alpha, p.astype(BF16), b)
        m = m_new
    a_prev, p_prev, b_prev = pending
    acc = a_prev * acc + _dot(vt_ref[:, b_prev[0]:b_prev[1]], p_prev)
    m_ref[...], l_ref[...], acc_ref[...] = m, l, acc

    @pl.when(ki == pl.num_programs(2) - 1)
    def _():
        o_ref[...] = (acc * (1.0 / l)).T.astype(o_ref.dtype)


def _attention(q, k, vt, heads):
    n = q.shape[0]
    nk = k.shape[0]
    qw = q.shape[1] // heads
    unit = MXU_DEPTH if nk % MXU_DEPTH == 0 else LANES
    ck = 3 * MXU_DEPTH
    tq = _tile(n, 2048, LANES)
    tk = _tile(nk, 13 * MXU_DEPTH, unit)
    return pl.pallas_call(
        functools.partial(_attn_body, ck=ck), grid=(heads, n // tq, nk // tk),
        in_specs=[pl.BlockSpec((tq, qw), lambda h, i, j: (i, h)),
                  pl.BlockSpec((tk, qw), lambda h, i, j: (j, h)),
                  pl.BlockSpec((V_DIM, tk), lambda h, i, j: (h, j))],
        out_specs=pl.BlockSpec((tq, V_DIM), lambda h, i, j: (i, h)),
        out_shape=jax.ShapeDtypeStruct((n, heads * V_DIM), BF16),
        scratch_shapes=[pltpu.VMEM((1, tq), F32), pltpu.VMEM((1, tq), F32), pltpu.VMEM((V_DIM, tq), F32)],
        name="mla_flash_attention",
        compiler_params=_params(("parallel", "parallel", "arbitrary")),
    )(q, k, vt)


def _vt_body(w_ref, c_ref, o_ref):
    o_ref[...] = lax.dot_general(w_ref[...], c_ref[...], (((1,), (1,)), ((), ())),
                                 preferred_element_type=F32).astype(o_ref.dtype)


def _v_up_transposed(w_vt, ckv):
    hv, r = w_vt.shape
    nk = ckv.shape[0]
    th = _tile(hv, 2 * V_DIM, V_DIM)
    tn = _tile(nk, 13 * MXU_DEPTH, LANES)
    return pl.pallas_call(
        _vt_body, grid=(hv // th, nk // tn),
        in_specs=[pl.BlockSpec((th, r), lambda i, j: (i, 0)), pl.BlockSpec((tn, r), lambda i, j: (j, 0))],
        out_specs=pl.BlockSpec((th, tn), lambda i, j: (i, j)),
        out_shape=jax.ShapeDtypeStruct((hv, nk), BF16), name="mla_v_up_transposed",
        compiler_params=_params(("parallel", "parallel")),
    )(w_vt, ckv)


def _filter_body(z_ref, w1_ref, w2_ref, w3_ref, bf_ref, w4_ref, dl_ref, o_ref, h_ref):
    @pl.when(pl.program_id(1) == 0)
    def _():
        bf = bf_ref[...]
        h = jnp.sin(bf[3:4, :] * (_dot3(z_ref[...], w1_ref[...]) + bf[0:1, :]))
        h = jnp.sin(bf[4:5, :] * (_dot3(h, w2_ref[...]) + bf[1:2, :]))
        h_ref[...] = jnp.sin(bf[5:6, :] * (_dot3(h, w3_ref[...]) + bf[2:3, :]))

    z = z_ref[...]
    t = z[:, 0:1]
    sign = z[:, HY_BANDS * 2 + 1:HY_BANDS * 2 + 2]
    filt = _dot(h_ref[...].astype(BF16), w4_ref[...].astype(BF16))
    o_ref[...] = (sign * filt * jnp.exp(-t * dl_ref[...])).astype(o_ref.dtype)


def _hyena_filter(n, d, f_w1, f_b1, f_w2, f_b2, f_w3, f_b3, f_freq, f_w4):
    fh = f_w1.shape[1]
    emb = f_w1.shape[0]
    r = jnp.arange(2 * n, dtype=jnp.int32)
    p = jnp.minimum(jnp.where(r < n, r, 2 * n - r), n - 1)
    t = jnp.linspace(0.0, 1.0, n, dtype=F32)[p][:, None]
    w = ((2.0 * math.pi / n) * jnp.arange(n, dtype=F32))[p][:, None]
    bands = jnp.linspace(1e-4, HY_BANDS - 1, HY_BANDS, dtype=F32)
    sign = jnp.where(r < n, 1.0, jnp.where(r == n, 0.0, -1.0)).astype(F32)[:, None]
    zw = LANES // 2
    z = jnp.concatenate([t, jnp.cos(bands * w), -jnp.sin(bands * w), sign,
                         jnp.zeros((2 * n, zw - emb - 1), F32)], axis=-1)
    w1p = jnp.concatenate([f_w1.astype(F32), jnp.zeros((zw - emb, fh), F32)], axis=0)
    bf = jnp.concatenate([f_b1.reshape(1, fh), f_b2.reshape(1, fh), f_b3.reshape(1, fh),
                          f_freq.reshape(3, fh), jnp.zeros((2, fh), F32)], axis=0).astype(F32)
    deltas = jnp.abs(jnp.linspace(math.log(HY_DECAY_TARGET) / HY_SLOW_DECAY_PCT,
                                  math.log(HY_DECAY_TARGET) / HY_FAST_DECAY_PCT, d, dtype=F32)).reshape(1, d)
    tr = _tile(n, 512, SUBLANES)
    tc = _tile(d, 1024, LANES)
    ncb = d // tc
    half = n // tr
    const = lambda shape: pl.BlockSpec(shape, lambda i, j: (0, 0))
    return pl.pallas_call(
        _filter_body, grid=(2 * n // tr, ncb),
        in_specs=[pl.BlockSpec((tr, zw), lambda i, j: (i, 0)), const((zw, fh)), const((fh, fh)), const((fh, fh)),
                  const((SUBLANES, fh)),
                  pl.BlockSpec((fh, tc), lambda i, j: (0, j + jnp.where(i >= half, ncb, 0))),
                  pl.BlockSpec((1, tc), lambda i, j: (0, j))],
        out_specs=pl.BlockSpec((tr, tc), lambda i, j: (i, j)),
        out_shape=jax.ShapeDtypeStruct((2 * n, d), BF16),
        scratch_shapes=[pltpu.VMEM((tr, fh), F32)], name="hyena_filter",
        compiler_params=_params(("parallel", "arbitrary")),
    )(z, w1p, f_w2.astype(F32), f_w3.astype(F32), bf, f_w4.astype(F32), deltas)


def _dft_tables(n):
    big_n = 2 * n
    n2 = DFT_INNER
    n1 = big_n // n2
    hk = n1 // 2
    k1 = jnp.arange(hk, dtype=jnp.int32)
    m1 = jnp.arange(n1, dtype=jnp.int32)
    ang_a = (math.pi / n1) * ((m1[None, :] * (2 * k1[:, None] + 1)) % (2 * n1)).astype(F32)
    fwd = jnp.concatenate([jnp.cos(ang_a), -jnp.sin(ang_a)], axis=0)
    inv = (2.0 / big_n) * jnp.concatenate([jnp.cos(ang_a[:, :hk]).T, -jnp.sin(ang_a[:, :hk]).T], axis=1)
    k2 = jnp.arange(n2, dtype=jnp.int32)
    m2 = jnp.arange(n2, dtype=jnp.int32)
    freq = 2 * k1[:, None, None] + 1 + 2 * n1 * k2[None, :, None]
    ang_c = (math.pi / big_n) * ((m2[None, None, :] * freq) % (2 * big_n)).astype(F32)
    gr, gi = jnp.cos(ang_c), -jnp.sin(ang_c)
    mid = jnp.concatenate([jnp.concatenate([gr, -gi], axis=2), jnp.concatenate([gi, gr], axis=2)], axis=1)
    mid_t = jnp.swapaxes(mid, 1, 2)
    return fwd, inv, mid, mid_t


def _dft_rows_body(f_ref, x_ref, o_ref):
    o_ref[...] = _dot(f_ref[...], x_ref[...]).astype(o_ref.dtype)


def _dft_rows(f, x2d, tn):
    r, k = f.shape
    c = x2d.shape[1]
    return pl.pallas_call(
        _dft_rows_body, grid=(c // tn,),
        in_specs=[pl.BlockSpec((r, k), lambda j: (0, 0)), pl.BlockSpec((k, tn), lambda j: (0, j))],
        out_specs=pl.BlockSpec((r, tn), lambda j: (0, j)),
        out_shape=jax.ShapeDtypeStruct((r, c), BF16), name="hyena_dft_outer",
        compiler_params=_params(("parallel",)),
    )(f.astype(BF16), x2d)


def _stack_re_im(ref):
    _, _, n2, tc = ref.shape
    return ref[:, 0].reshape(2 * n2, tc)


def _dft_mid_body(m_ref, a_ref, o_ref):
    n2 = a_ref.shape[2]
    s = _dot(m_ref[0], _stack_re_im(a_ref))
    o_ref[0, 0] = s[:n2].astype(o_ref.dtype)
    o_ref[1, 0] = s[n2:].astype(o_ref.dtype)


def _dft_conv_body(m_ref, t_ref, a_ref, g_ref, o_ref):
    n2 = a_ref.shape[2]
    s = _dot(m_ref[0], _stack_re_im(a_ref))
    sr, si = s[:n2], s[n2:]
    gr, gi = g_ref[0, 0].astype(F32), g_ref[1, 0].astype(F32)
    y = jnp.concatenate([sr * gr - si * gi, sr * gi + si * gr], axis=0).astype(BF16)
    b = _dot(t_ref[0], y)
    o_ref[0, 0] = b[:n2].astype(o_ref.dtype)
    o_ref[1, 0] = b[n2:].astype(o_ref.dtype)


def _dft_mid(mid, a4, spec4=None, mid_t=None):
    _, hk, n2, d = a4.shape
    tc = _tile(d, 1024, LANES)
    mspec = pl.BlockSpec((1, 2 * n2, 2 * n2), lambda k, j: (k, 0, 0))
    dspec = pl.BlockSpec((2, 1, n2, tc), lambda k, j: (0, k, 0, j))
    if spec4 is None:
        body, in_specs, operands, name = _dft_mid_body, [mspec, dspec], [mid.astype(BF16), a4], "hyena_dft_inner"
    else:
        body, in_specs = _dft_conv_body, [mspec, mspec, dspec, dspec]
        operands, name = [mid.astype(BF16), mid_t.astype(BF16), a4, spec4], "hyena_dft_inner_conv"
    return pl.pallas_call(
        body, grid=(hk, d // tc), in_specs=in_specs, out_specs=dspec,
        out_shape=jax.ShapeDtypeStruct(a4.shape, BF16), name=name,
        compiler_params=_params(("parallel", "arbitrary")),
    )(*operands)


def _row_tile(m):
    return _tile(m, 1024, 16)


def _short_conv(h, w_in, w_conv, w_out):
    m, d = h.shape
    tm = _row_tile(m)
    tn_in = _tile(d, 256, LANES)
    ncb = d // tn_in
    b, u = _matmul(h, [(w_in, 0), (w_in, ncb), (w_in, 2 * ncb)], [(d, tn_in, BF16), (d, tn_in, BF16)],
                   _ep_gate_pair, tm=tm, tn=tn_in, name="sc_in_proj")
    tn = _tile(d, 512, LANES)
    tmo = _tile(m, 512, HALO_ROWS)
    w8 = jnp.concatenate([w_conv.astype(F32), jnp.zeros((SUBLANES - 3, d), F32)], axis=0)
    main, prev, nxt = _row_halo_specs(tmo, d, m)
    lhs = [(b, *main), (u, *main), (u, *prev), (u, *nxt), (w8, (SUBLANES, d), lambda i, j, k: (0, 0))]
    return _matmul(lhs, [(w_out, 0)], [(d, tn, BF16)], _ep_plain, tm=tmo, tn=tn, name="sc_out_proj",
                   lhs_fn=_lhs_sc_gate, lhs_shape=(m, d))[0]


def _ffn(h2, w_gate_up, w_down):
    m, d = h2.shape
    f = w_down[0].shape[1]
    tm = _row_tile(m)
    tf = _tile(f, 512, LANES)
    a = _matmul(h2, [(w_gate_up, 0), (w_gate_up, f // tf)], [(f, tf, BF16)], _ep_swiglu,
                tm=_tile(m, 2048, 16), tn=tf, name="ffn_gate_up")[0]
    tk = f if f <= 4096 else _tile(f, 6144, LANES)
    tn = _tile(d, 512, LANES)
    return _matmul(a, [(w_down, 0)], [(d, tn, BF16)], _ep_plain, tm=tm, tn=tn, tk=tk, name="ffn_down")[0]


def _rope_tables(n, n_ctx, scale):
    rows = n // GRID_W
    row = jnp.repeat(jnp.arange(rows, dtype=F32), GRID_W)
    col = jnp.tile(jnp.arange(GRID_W, dtype=F32), rows)
    axis_dim = ROPE_DIM // 2
    inv = ROPE_BASE ** (-jnp.arange(0, axis_dim, 2, dtype=F32) / axis_dim)
    ang_r, ang_c = row[:, None] * inv, col[:, None] * inv
    cos = jnp.concatenate([jnp.cos(ang_r), jnp.cos(ang_c)], axis=1)
    sin = jnp.concatenate([jnp.sin(ang_r), jnp.sin(ang_c)], axis=1)
    rot = jnp.concatenate([cos, cos, -sin, sin], axis=1)
    tab_q = scale * jnp.concatenate([jnp.ones((n, NOPE_DIM), F32), rot], axis=1)
    ctx_rot = jnp.concatenate([jnp.ones((n_ctx, ROPE_DIM), F32), jnp.zeros((n_ctx, ROPE_DIM), F32)], axis=1)
    return tab_q, rot, ctx_rot


def _mla(h, hc, w_down, q_norm, kv_norm, w_uq, w_ukv, w_out):
    n, d = h.shape
    n_ctx = hc.shape[0]
    q_rank, kv_rank = w_uq.shape[0], w_ukv.shape[0]
    heads = w_uq.shape[1] // (NOPE_DIM + ROPE_DIM)
    scale = (NOPE_DIM + ROPE_DIM) ** -0.5 * math.log2(math.e)
    tab_q, tab_k, tab_kc = _rope_tables(n, n_ctx, scale)

    def rope_cols(w):
        qd = ROPE_DIM // 4
        a = jnp.concatenate([w[..., 0:qd], w[..., 2 * qd:3 * qd]], axis=-1)
        b = jnp.concatenate([w[..., qd:2 * qd], w[..., 3 * qd:4 * qd]], axis=-1)
        return jnp.concatenate([a, b, b, a], axis=-1)

    w_dq = w_down[:, :q_rank].astype(BF16)
    w_dkv = jnp.concatenate([w_down[:, q_rank:q_rank + kv_rank], rope_cols(w_down[:, q_rank + kv_rank:])],
                            axis=1).astype(BF16)
    wq3 = w_uq.reshape(q_rank, heads, NOPE_DIM + ROPE_DIM)
    w_uq_p = jnp.concatenate([wq3[:, :, :NOPE_DIM], rope_cols(wq3[:, :, NOPE_DIM:])], axis=2)
    qw = NOPE_DIM + 2 * ROPE_DIM
    w_uq_p = w_uq_p.reshape(q_rank, heads * qw).astype(BF16)
    wkv3 = w_ukv.reshape(kv_rank, heads, NOPE_DIM + V_DIM)
    w_kn = wkv3[:, :, :NOPE_DIM].reshape(kv_rank, heads * NOPE_DIM).astype(BF16)
    w_vt = wkv3[:, :, NOPE_DIM:].reshape(kv_rank, heads * V_DIM).T.astype(BF16)

    tm = _row_tile(n)
    cqn = _matmul(h, [(w_dq, 0)], [(q_rank, q_rank, BF16)], _ep_rms,
                  extras=[(q_norm.reshape(1, q_rank).astype(F32), (1, q_rank), lambda i, j, k: (0, 0))],
                  tm=tm, tn=q_rank, name="mla_q_down")[0]

    def kv_down(hh, tab, name):
        m = hh.shape[0]
        tmk = _row_tile(m)
        wd = kv_rank + 2 * ROPE_DIM
        return _matmul(hh, [(w_dkv, 0)], [(kv_rank, kv_rank, BF16), (LANES, LANES, BF16)],
                       functools.partial(_ep_kv_down, kv_rank=kv_rank),
                       extras=[(kv_norm.reshape(1, kv_rank).astype(F32), (1, kv_rank), lambda i, j, k: (0, 0)),
                               (tab, (tmk, 2 * ROPE_DIM), lambda i, j, k: (i, 0))],
                       tm=tmk, tn=wd, name=name)

    ckv, kr = kv_down(h, tab_k, "mla_kv_down")
    ckv_c, kr_c = kv_down(hc, tab_kc, "mla_kv_down_ctx")
    ckv = jnp.concatenate([ckv, ckv_c], axis=0)
    kr = jnp.concatenate([kr, kr_c], axis=0)
    nk = n + n_ctx

    gq = 2 if heads % 2 == 0 else 1
    q = _matmul(cqn, [(w_uq_p, 0)], [(heads * qw, gq * qw, BF16)], _ep_q_up,
                extras=[(tab_q, (tm, qw), lambda i, j, k: (i, 0))], tm=tm, tn=gq * qw, name="mla_q_up")[0]
    tmk = _tile(nk, 1664, 16)
    k = _matmul(ckv, [(w_kn, 0)], [(heads * qw, gq * qw, BF16)], _ep_k_up,
                extras=[(kr, (tmk, LANES), lambda i, j, k: (i, 0))], tm=tmk, tn=gq * NOPE_DIM, name="mla_k_up")[0]
    vt = _v_up_transposed(w_vt, ckv)
    o = _attention(q, k, vt, heads)
    tn = _tile(d, 512, LANES)
    return _matmul(o, [(w_out, 0)], [(d, tn, BF16)], _ep_plain, tm=tm, tn=tn, name="mla_out_proj")[0]


def _hyena(h, w_in, conv_w, conv_b, f_w1, f_b1, f_w2, f_b2, f_w3, f_b3, f_freq, f_w4, skip, w_out):
    n, d = h.shape
    tm = _row_tile(n)
    tn = _tile(d, 512, LANES)
    z = _matmul(h, [(w_in, 0)], [(3 * d, tn, BF16)], _ep_plain, tm=tm, tn=tn, name="hy_in_proj")[0]
    x0, vv = _hy_gate(z, conv_w, conv_b)
    n2 = DFT_INNER
    n1 = 2 * n // n2
    hk = n1 // 2
    fwd, inv, mid, mid_t = _dft_tables(n)
    g = _hyena_filter(n, d, f_w1, f_b1, f_w2, f_b2, f_w3, f_b3, f_freq, f_w4)
    tcol = _tile(n2 * d, 4096, LANES)
    ga = _dft_rows(fwd, g.reshape(n1, n2 * d), tcol)
    spec = _dft_mid(mid, ga.reshape(2, hk, n2, d))
    va = _dft_rows(fwd[:, :hk], vv.reshape(hk, n2 * d), tcol)
    vb = _dft_mid(mid, va.reshape(2, hk, n2, d), spec, mid_t)
    y = _dft_rows(inv, vb.reshape(n1, n2 * d), tcol).reshape(n, d)
    tmo = _tile(n, 512, 16)
    row = ((tmo, d), lambda i, j, k: (i, 0))
    lhs = [(x0, *row), (y, *row), (vv, *row), (skip.reshape(1, d).astype(F32), (1, d), lambda i, j, k: (0, 0))]
    return _matmul(lhs, [(w_out, 0)], [(d, tn, BF16)], _ep_plain, tm=tmo, tn=tn, name="hy_out_proj",
                   lhs_fn=_lhs_hy_gate, lhs_shape=(n, d))[0]


def kernel(x, c, ctx, c_ctx, ada_down, ada_up, ada_bias, norm_gain, ffn_w_gate_up, ffn_w_down, sc_w_in, sc_conv, sc_w_out, mla_w_down, mla_q_norm, mla_kv_norm, mla_w_uq, mla_w_ukv, mla_w_out, hy_w_in, hy_conv, hy_conv_b, hy_f_w1, hy_f_b1, hy_f_w2, hy_f_b2, hy_f_w3, hy_f_b3, hy_f_freq, hy_f_w4, hy_skip, hy_w_out):
    batch, n, d = x.shape
    assert batch == 1 and c.shape[0] == 1 and ctx.shape[0] == 1
    depth = ada_down.shape[0]
    n_mixers = 3
    xs = x.reshape(n, d)
    cs = ctx.reshape(ctx.shape[1], d)

    mla_layers = [i for i in range(depth) if i % n_mixers == 1]
    last_ctx_read = mla_layers[-1] if mla_layers else -1

    s_raw = jnp.concatenate([c.reshape(1, d), c_ctx.reshape(1, d), jnp.zeros((2 * SUBLANES - 2, d), F32)], axis=0)
    mods = _adaln(s_raw, ada_down, ada_up, ada_bias)

    def mod_vecs(i, row):
        return [mods[i, row, m * d:(m + 1) * d] for m in range(N_MOD)]

    ffn_gu, ffn_dn = ffn_w_gate_up.astype(BF16), ffn_w_down.astype(BF16)
    sc_in, sc_out = sc_w_in.astype(BF16), sc_w_out.astype(BF16)
    hy_in, hy_out = hy_w_in.astype(BF16), hy_w_out.astype(BF16)
    mla_out = mla_w_out.astype(BF16)
    pend = None
    pend_c = None
    for i in range(depth):
        kind, j = i % n_mixers, i // n_mixers
        ctx_full = i < last_ctx_read
        ctx_keys = i == last_ctx_read
        g = norm_gain[i]
        streams = [(0, xs, pend)]
        if ctx_full or ctx_keys:
            streams.append((1, cs, pend_c))
        hs = {}
        cur = {}
        for row, xv, pd in streams:
            mv = mod_vecs(i, row)
            if pd is None:
                _, hh = _resid_norm_mod(xv, None, _vec_rows(d, g[0], g[0], g[0], mv[0], mv[1]), has_h=True)
            else:
                xv, hh = _resid_norm_mod(xv, pd[0], _vec_rows(d, pd[1], pd[2], g[0], mv[0], mv[1]), has_h=True)
            hs[row], cur[row] = hh, xv

        ys = {}
        if kind == 0:
            w_in, w_out = (sc_in, j), (sc_out, j)
            ys[0] = _short_conv(hs[0], w_in, sc_conv[j], w_out)
            if ctx_full:
                ys[1] = _short_conv(hs[1], w_in, sc_conv[j], w_out)
        elif kind == 1:
            ys[0] = _mla(hs[0], hs[1], mla_w_down[j], mla_q_norm[j], mla_kv_norm[j], mla_w_uq[j], mla_w_ukv[j],
                         (mla_out, j))
            assert not ctx_full
        else:
            hp = ((hy_in, j), hy_conv[j], hy_conv_b[j], hy_f_w1[j], hy_f_b1[j], hy_f_w2[j], hy_f_b2[j],
                  hy_f_w3[j], hy_f_b3[j], hy_f_freq[j], hy_f_w4[j], hy_skip[j], (hy_out, j))
            ys[0] = _hyena(hs[0], *hp)
            if ctx_full:
                ys[1] = _hyena(hs[1], *hp)

        w_gu, w_dn = (ffn_gu, i), (ffn_dn, i)
        new_pend = {0: None, 1: None}
        for row in ys:
            mv = mod_vecs(i, row)
            xv, h2 = _resid_norm_mod(cur[row], ys[row], _vec_rows(d, mv[2], g[1], g[2], mv[3], mv[4]), has_h=True)
            cur[row] = xv
            new_pend[row] = (_ffn(h2, w_gu, w_dn), mv[5], g[3])
        xs, pend = cur[0], new_pend[0]
        if ctx_full:
            cs, pend_c = cur[1], new_pend[1]
        else:
            pend_c = None

    xs, _ = _resid_norm_mod(xs, pend[0], _vec_rows(d, pend[1], pend[2], pend[2], pend[1], pend[1]), has_h=False)
    return xs.reshape(batch, n, d)
```

```python
import functools
import math

import jax
import jax.numpy as jnp
from jax import lax
from jax.experimental import pallas as pl
from jax.experimental.pallas import tpu as pltpu

F32 = jnp.float32
BF16 = jnp.bfloat16

EPS = 1e-6
N_MOD = 6
NOPE_DIM = 128
ROPE_DIM = 64
V_DIM = 128
GRID_W = 64
ROPE_BASE = 10000.0
HY_BANDS = 16
HY_DECAY_TARGET = 1e-2
HY_FAST_DECAY_PCT = 0.3
HY_SLOW_DECAY_PCT = 1.5

LANES = 128
SUBLANES = 8
VMEM_LIMIT_BYTES = 56 * 1024 * 1024
DFT_INNER = 128
HALO_ROWS = 16
MXU_DEPTH = 256

def _tile(dim, pref, align):
    best = None
    t = align
    while t <= min(dim, pref):
        if dim % t == 0:
            best = t
        t += align
    return best if best is not None else dim


def _params(sem):
    return pltpu.CompilerParams(dimension_semantics=sem, vmem_limit_bytes=VMEM_LIMIT_BYTES)


def _split_hi_lo(x):
    hi = x.astype(BF16)
    lo = (x - hi.astype(F32)).astype(BF16)
    return hi, lo


def _dot(a, b):
    return jnp.dot(a, b, preferred_element_type=F32)


def _dot3(a, b):
    ah, al = _split_hi_lo(a)
    bh, bl = _split_hi_lo(b)
    return _dot(ah, bh) + _dot(ah, bl) + _dot(al, bh)


def _rms(x, gain):
    return x * lax.rsqrt(jnp.mean(x * x, axis=-1, keepdims=True) + EPS) * gain


def _silu(x):
    return x * (1.0 / (1.0 + jnp.exp(-x)))


def _mm_body(*refs, n_lhs, n_w, n_extra, n_out, nk, epilogue, lhs_fn):
    refs = list(refs)
    lhs_refs = [refs.pop(0) for _ in range(n_lhs)]
    w_refs = [refs.pop(0) for _ in range(n_w)]
    extra_refs = [refs.pop(0) for _ in range(n_extra)]
    out_refs = [refs.pop(0) for _ in range(n_out)]
    acc_refs = [refs.pop(0) for _ in range(n_w if nk > 1 else 0)]
    if lhs_fn is None:
        a = lhs_refs[0][...].astype(BF16)
    else:
        lhs_scratch = refs.pop(0)

        @pl.when(pl.program_id(1) == 0)
        def _():
            lhs_fn(lhs_refs, lhs_scratch)

        a = lhs_scratch[...]
    dots = [_dot(a, w[...].astype(BF16)) for w in w_refs]
    if nk == 1:
        epilogue(dots, extra_refs, out_refs)
        return
    k = pl.program_id(2)

    @pl.when(k == 0)
    def _():
        for acc, d in zip(acc_refs, dots):
            acc[...] = d

    @pl.when(k > 0)
    def _():
        for acc, d in zip(acc_refs, dots):
            acc[...] += d

    @pl.when(k == nk - 1)
    def _():
        epilogue([acc[...] for acc in acc_refs], extra_refs, out_refs)


def _matmul(a, ws, outs, epilogue, *, extras=(), tm, tn, tk=None, name, lhs_fn=None, lhs_shape=None):
    if lhs_fn is None:
        M, K = a.shape
    else:
        M, K = lhs_shape
    tk = K if tk is None else tk
    nk = K // tk
    assert lhs_fn is None or nk == 1
    n_col_blocks = outs[0][0] // outs[0][1]
    grid = (M // tm, n_col_blocks, nk)
    if lhs_fn is None:
        in_specs = [pl.BlockSpec((tm, tk), lambda i, j, k: (i, k))]
        operands = [a]
    else:
        in_specs = [pl.BlockSpec(bshape, imap) for _, bshape, imap in a]
        operands = [arr for arr, _, _ in a]
    n_lhs = len(operands)
    for w, off in ws:
        if isinstance(w, tuple):
            w, layer = w
            in_specs.append(pl.BlockSpec((None, tk, tn),
                                         functools.partial(lambda i, j, k, o, l: (l, k, j + o), o=off, l=layer)))
        else:
            in_specs.append(pl.BlockSpec((tk, tn), functools.partial(lambda i, j, k, o: (k, j + o), o=off)))
        operands.append(w)
    for arr, bshape, imap in extras:
        in_specs.append(pl.BlockSpec(bshape, imap))
        operands.append(arr)
    out_shape = [jax.ShapeDtypeStruct((M, wt), dt) for wt, _, dt in outs]
    out_specs = [pl.BlockSpec((tm, bw), lambda i, j, k: (i, j)) for _, bw, _ in outs]
    scratch = [pltpu.VMEM((tm, tn), F32) for _ in ws] if nk > 1 else []
    if lhs_fn is not None:
        scratch.append(pltpu.VMEM((tm, K), BF16))
    body = functools.partial(_mm_body, n_lhs=n_lhs, n_w=len(ws), n_extra=len(extras), n_out=len(outs), nk=nk,
                             epilogue=epilogue, lhs_fn=lhs_fn)
    res = pl.pallas_call(
        body, grid=grid, in_specs=in_specs, out_specs=out_specs, out_shape=out_shape,
        scratch_shapes=scratch, name=name,
        compiler_params=_params(("parallel", "arbitrary", "arbitrary")),
    )(*operands)
    return res


def _ep_plain(dots, extras, outs):
    outs[0][...] = dots[0].astype(outs[0].dtype)


def _ep_gate_pair(dots, extras, outs):
    outs[0][...] = dots[0].astype(outs[0].dtype)
    outs[1][...] = (dots[1] * dots[2]).astype(outs[1].dtype)


def _ep_swiglu(dots, extras, outs):
    outs[0][...] = (_silu(dots[0]) * dots[1]).astype(outs[0].dtype)


def _ep_rms(dots, extras, outs):
    outs[0][...] = _rms(dots[0], extras[0][...]).astype(outs[0].dtype)


def _ep_kv_down(dots, extras, outs, *, kv_rank):
    gain_ref, tab_ref = extras
    d = dots[0]
    outs[0][...] = _rms(d[:, :kv_rank], gain_ref[...]).astype(outs[0].dtype)
    p = d[:, kv_rank:] * tab_ref[...]
    outs[1][...] = (p + pltpu.roll(p, ROPE_DIM, axis=1)).astype(outs[1].dtype)


def _ep_q_up(dots, extras, outs):
    tab = extras[0][...]
    x = dots[0]
    width = tab.shape[1]
    for g in range(x.shape[1] // width):
        outs[0][:, g * width:(g + 1) * width] = (x[:, g * width:(g + 1) * width] * tab).astype(outs[0].dtype)


def _ep_k_up(dots, extras, outs):
    kn = dots[0]
    kr = extras[0][...]
    for g in range(kn.shape[1] // NOPE_DIM):
        base = g * (NOPE_DIM + LANES)
        outs[0][:, base:base + NOPE_DIM] = kn[:, g * NOPE_DIM:(g + 1) * NOPE_DIM].astype(outs[0].dtype)
        outs[0][:, base + NOPE_DIM:base + NOPE_DIM + LANES] = kr


def _adaln_body(s_ref, down_ref, up_ref, bias_ref, out_ref, t_ref):
    @pl.when(pl.program_id(1) == 0)
    def _():
        t_ref[...] = _dot3(_silu(s_ref[...]), down_ref[0])

    out_ref[0] = _dot3(t_ref[...], up_ref[0]) + bias_ref[0]


def _adaln(s_raw, ada_down, ada_up, ada_bias):
    depth, d, r = ada_down.shape
    n6 = ada_up.shape[2]
    rows = s_raw.shape[0]
    tn = _tile(n6, 2048, LANES)
    return pl.pallas_call(
        _adaln_body, grid=(depth, n6 // tn),
        in_specs=[pl.BlockSpec((rows, d), lambda i, j: (0, 0)),
                  pl.BlockSpec((1, d, r), lambda i, j: (i, 0, 0)),
                  pl.BlockSpec((1, r, tn), lambda i, j: (i, 0, j)),
                  pl.BlockSpec((1, 1, tn), lambda i, j: (i, 0, j))],
        out_specs=pl.BlockSpec((1, rows, tn), lambda i, j: (i, 0, j)),
        out_shape=jax.ShapeDtypeStruct((depth, rows, n6), F32),
        scratch_shapes=[pltpu.VMEM((rows, r), F32)], name="adaln",
        compiler_params=_params(("arbitrary", "arbitrary")),
    )(s_raw, ada_down, ada_up, ada_bias.reshape(depth, 1, n6))


def _rnm_body(*refs, has_resid, has_h):
    refs = list(refs)
    x_ref = refs.pop(0)
    y_ref = refs.pop(0) if has_resid else None
    vec_ref = refs.pop(0)
    x = x_ref[...]
    if has_resid:
        xo_ref = refs.pop(0)
        x = x + vec_ref[0:1, :] * _rms(y_ref[...].astype(F32), vec_ref[1:2, :])
        xo_ref[...] = x
    if has_h:
        h_ref = refs.pop(0)
        h_ref[...] = (_rms(x, vec_ref[2:3, :]) * (1.0 + vec_ref[4:5, :]) + vec_ref[3:4, :]).astype(h_ref.dtype)


def _resid_norm_mod(x, y, vec, *, has_h, rows=None):
    n, d = x.shape
    n = n if rows is None else rows
    has_resid = y is not None
    tr = _tile(n, 256, 16)
    spec = pl.BlockSpec((tr, d), lambda i: (i, 0))
    in_specs = [spec] + ([spec] if has_resid else []) + [pl.BlockSpec(vec.shape, lambda i: (0, 0))]
    operands = [x] + ([y] if has_resid else []) + [vec]
    out_shape, out_specs = [], []
    if has_resid:
        out_shape.append(jax.ShapeDtypeStruct((n, d), F32))
        out_specs.append(spec)
    if has_h:
        out_shape.append(jax.ShapeDtypeStruct((n, d), BF16))
        out_specs.append(spec)
    res = pl.pallas_call(
        functools.partial(_rnm_body, has_resid=has_resid, has_h=has_h), grid=(n // tr,),
        in_specs=in_specs, out_specs=out_specs, out_shape=out_shape, name="resid_norm_mod",
        compiler_params=_params(("parallel",)),
    )(*operands)
    res = list(res)
    x_new = res.pop(0) if has_resid else None
    h = res.pop(0) if has_h else None
    return x_new, h


def _vec_rows(d, *rows):
    out = [r.reshape(1, d).astype(F32) for r in rows]
    out += [jnp.zeros((1, d), F32)] * (SUBLANES - len(out))
    return jnp.concatenate(out, axis=0)


def _conv3_rows(main, prev_blk, next_blk, w0, w1, w2, is_first, is_last):
    main = main.astype(F32)
    tr = main.shape[0]
    rows = lax.broadcasted_iota(jnp.int32, main.shape, 0)
    prev_row = jnp.where(is_first, 0.0, prev_blk[HALO_ROWS - 1:HALO_ROWS, :].astype(F32))
    next_row = jnp.where(is_last, 0.0, next_blk[0:1, :].astype(F32))
    up = jnp.where(rows == 0, prev_row, pltpu.roll(main, 1, axis=0))
    dn = jnp.where(rows == tr - 1, next_row, pltpu.roll(main, tr - 1, axis=0))
    return w0 * up + w1 * main + w2 * dn


def _halo_specs(tr, tc, n_rows, col_off):
    per = tr // HALO_ROWS
    last = n_rows // HALO_ROWS - 1
    return [
        pl.BlockSpec((tr, tc), lambda i, j: (i, j + col_off)),
        pl.BlockSpec((HALO_ROWS, tc), lambda i, j: (jnp.maximum(i * per - 1, 0), j + col_off)),
        pl.BlockSpec((HALO_ROWS, tc), lambda i, j: (jnp.minimum((i + 1) * per, last), j + col_off)),
    ]


LHS_CHUNK = 512


def _lhs_sc_gate(refs, scratch):
    b_ref, u_ref, up_ref, un_ref, w_ref = refs
    i = pl.program_id(0)
    first, last = i == 0, i == pl.num_programs(0) - 1
    width = scratch.shape[1]
    cw = min(width, LHS_CHUNK)
    for c in range(0, width, cw):
        sl = slice(c, c + cw)
        conv = _conv3_rows(u_ref[:, sl], up_ref[:, sl], un_ref[:, sl], w_ref[0:1, sl], w_ref[1:2, sl],
                           w_ref[2:3, sl], first, last)
        scratch[:, sl] = (b_ref[:, sl].astype(F32) * conv).astype(scratch.dtype)


def _lhs_hy_gate(refs, scratch):
    x0_ref, y_ref, v_ref, skip_ref = refs
    width = scratch.shape[1]
    cw = min(width, LHS_CHUNK)
    for c in range(0, width, cw):
        sl = slice(c, c + cw)
        yv = y_ref[:, sl].astype(F32) + v_ref[:, sl].astype(F32) * skip_ref[:, sl]
        scratch[:, sl] = (x0_ref[:, sl].astype(F32) * yv).astype(scratch.dtype)


def _row_halo_specs(tm, d, n_rows):
    per = tm // HALO_ROWS
    last = n_rows // HALO_ROWS - 1
    return [
        ((tm, d), lambda i, j, k: (i, 0)),
        ((HALO_ROWS, d), lambda i, j, k: (jnp.maximum(i * per - 1, 0), 0)),
        ((HALO_ROWS, d), lambda i, j, k: (jnp.minimum((i + 1) * per, last), 0)),
    ]


def _hy_gate_body(*refs):
    z_refs = refs[0:9]
    w_refs = refs[9:12]
    x0_ref, vv_ref = refs[12:14]
    i = pl.program_id(0)
    first, last = i == 0, i == pl.num_programs(0) - 1
    conv = []
    for g in range(3):
        m, p, nx = z_refs[3 * g:3 * g + 3]
        w = w_refs[g]
        conv.append(_conv3_rows(m[...], p[...], nx[...], w[0:1, :], w[1:2, :], w[2:3, :], first, last) + w[3:4, :])
    x0_ref[...] = conv[0].astype(x0_ref.dtype)
    vv_ref[...] = (conv[1] * conv[2]).astype(vv_ref.dtype)


def _hy_gate(z, conv_w, conv_b):
    n, d3 = z.shape
    d = d3 // 3
    tr = _tile(n, 512, 16)
    tc = _tile(d, 512, LANES)
    ncb = d // tc
    w8 = jnp.concatenate([conv_w.astype(F32), conv_b.reshape(1, d3).astype(F32),
                          jnp.zeros((SUBLANES - 4, d3), F32)], axis=0)
    in_specs, operands = [], []
    for g in range(3):
        in_specs += _halo_specs(tr, tc, n, g * ncb)
        operands += [z, z, z]
    for g in range(3):
        in_specs.append(pl.BlockSpec((SUBLANES, tc), functools.partial(lambda i, j, o: (0, j + o), o=g * ncb)))
        operands.append(w8)
    spec = pl.BlockSpec((tr, tc), lambda i, j: (i, j))
    return pl.pallas_call(
        _hy_gate_body, grid=(n // tr, ncb), in_specs=in_specs, out_specs=[spec, spec],
        out_shape=[jax.ShapeDtypeStruct((n, d), BF16)] * 2, name="hy_conv_gate",
        compiler_params=_params(("parallel", "parallel")),
    )(*operands)


def _attn_body(q_ref, k_ref, vt_ref, o_ref, m_ref, l_ref, acc_ref, *, ck):
    ki = pl.program_id(2)

    @pl.when(ki == 0)
    def _():
        m_ref[...] = jnp.full(m_ref.shape, -jnp.inf, F32)
        l_ref[...] = jnp.zeros(l_ref.shape, F32)
        acc_ref[...] = jnp.zeros(acc_ref.shape, F32)

    q = q_ref[...]
    m, l, acc = m_ref[...], l_ref[...], acc_ref[...]
    tk = k_ref.shape[0]
    bounds = [(lo, min(lo + ck, tk)) for lo in range(0, tk, ck)]

    def scores(b):
        return lax.dot_general(k_ref[b[0]:b[1], :], q, (((1,), (1,)), ((), ())), preferred_element_type=F32)

    s_next = scores(bounds[0])
    pending = None
    for c, b in enumerate(bounds):
        s = s_next
        if c + 1 < len(bounds):
            s_next = scores(bounds[c + 1])
        if pending is not None:
            a_prev, p_prev, b_prev = pending
            acc = a_prev * acc + _dot(vt_ref[:, b_prev[0]:b_prev[1]], p_prev)
        m_new = jnp.maximum(m, jnp.max(s, axis=0, keepdims=True))
        alpha = jnp.exp2(m - m_new)
        p = jnp.exp2(s - m_new)
        l = alpha * l + jnp.sum(p, axis=0, keepdims=True)
        pending = (alpha, p.astype(BF16), b)
        m = m_new
    a_prev, p_prev, b_prev = pending
    acc = a_prev * acc + _dot(vt_ref[:, b_prev[0]:b_prev[1]], p_prev)
    m_ref[...], l_ref[...], acc_ref[...] = m, l, acc

    @pl.when(ki == pl.num_programs(2) - 1)
    def _():
        o_ref[...] = (acc * (1.0 / l)).T.astype(o_ref.dtype)


def _attention(q, k, vt, heads):
    n = q.shape[0]
    nk = k.shape[0]
    qw = q.shape[1] // heads
    unit = MXU_DEPTH if nk % MXU_DEPTH == 0 else LANES
    ck = 3 * MXU_DEPTH
    tq = _tile(n, 2048, LANES)
    tk = _tile(nk, 13 * MXU_DEPTH, unit)
    return pl.pallas_call(
        functools.partial(_attn_body, ck=ck), grid=(heads, n // tq, nk // tk),
        in_specs=[pl.BlockSpec((tq, qw), lambda h, i, j: (i, h)),
                  pl.BlockSpec((tk, qw), lambda h, i, j: (j, h)),
                  pl.BlockSpec((V_DIM, tk), lambda h, i, j: (h, j))],
        out_specs=pl.BlockSpec((tq, V_DIM), lambda h, i, j: (i, h)),
        out_shape=jax.ShapeDtypeStruct((n, heads * V_DIM), BF16),
        scratch_shapes=[pltpu.VMEM((1, tq), F32), pltpu.VMEM((1, tq), F32), pltpu.VMEM((V_DIM, tq), F32)],
        name="mla_flash_attention",
        compiler_params=_params(("parallel", "parallel", "arbitrary")),
    )(q, k, vt)


def _vt_body(w_ref, c_ref, o_ref):
    o_ref[...] = lax.dot_general(w_ref[...], c_ref[...], (((1,), (1,)), ((), ())),
                                 preferred_element_type=F32).astype(o_ref.dtype)


def _v_up_transposed(w_vt, ckv):
    hv, r = w_vt.shape
    nk = ckv.shape[0]
    th = _tile(hv, 2 * V_DIM, V_DIM)
    tn = _tile(nk, 13 * MXU_DEPTH, LANES)
    return pl.pallas_call(
        _vt_body, grid=(hv // th, nk // tn),
        in_specs=[pl.BlockSpec((th, r), lambda i, j: (i, 0)), pl.BlockSpec((tn, r), lambda i, j: (j, 0))],
        out_specs=pl.BlockSpec((th, tn), lambda i, j: (i, j)),
        out_shape=jax.ShapeDtypeStruct((hv, nk), BF16), name="mla_v_up_transposed",
        compiler_params=_params(("parallel", "parallel")),
    )(w_vt, ckv)


def _filter_body(z_ref, w1_ref, w2_ref, w3_ref, bf_ref, w4_ref, dl_ref, o_ref, h_ref):
    @pl.when(pl.program_id(1) == 0)
    def _():
        bf = bf_ref[...]
        h = jnp.sin(bf[3:4, :] * (_dot3(z_ref[...], w1_ref[...]) + bf[0:1, :]))
        h = jnp.sin(bf[4:5, :] * (_dot3(h, w2_ref[...]) + bf[1:2, :]))
        h_ref[...] = jnp.sin(bf[5:6, :] * (_dot3(h, w3_ref[...]) + bf[2:3, :]))

    z = z_ref[...]
    t = z[:, 0:1]
    sign = z[:, HY_BANDS * 2 + 1:HY_BANDS * 2 + 2]
    filt = _dot(h_ref[...].astype(BF16), w4_ref[...].astype(BF16))
    o_ref[...] = (sign * filt * jnp.exp(-t * dl_ref[...])).astype(o_ref.dtype)


def _hyena_filter(n, d, f_w1, f_b1, f_w2, f_b2, f_w3, f_b3, f_freq, f_w4):
    fh = f_w1.shape[1]
    emb = f_w1.shape[0]
    r = jnp.arange(2 * n, dtype=jnp.int32)
    p = jnp.minimum(jnp.where(r < n, r, 2 * n - r), n - 1)
    pf = p.astype(F32)[:, None]
    t = pf / (n - 1)
    w = (2.0 * math.pi / n) * pf
    bands = jnp.linspace(1e-4, HY_BANDS - 1, HY_BANDS, dtype=F32)
    sign = jnp.where(r < n, 1.0, jnp.where(r == n, 0.0, -1.0)).astype(F32)[:, None]
    zw = LANES // 2
    z = jnp.concatenate([t, jnp.cos(bands * w), -jnp.sin(bands * w), sign,
                         jnp.zeros((2 * n, zw - emb - 1), F32)], axis=-1)
    w1p = jnp.concatenate([f_w1.astype(F32), jnp.zeros((zw - emb, fh), F32)], axis=0)
    bf = jnp.concatenate([f_b1.reshape(1, fh), f_b2.reshape(1, fh), f_b3.reshape(1, fh),
                          f_freq.reshape(3, fh), jnp.zeros((2, fh), F32)], axis=0).astype(F32)
    deltas = jnp.abs(jnp.linspace(math.log(HY_DECAY_TARGET) / HY_SLOW_DECAY_PCT,
                                  math.log(HY_DECAY_TARGET) / HY_FAST_DECAY_PCT, d, dtype=F32)).reshape(1, d)
    tr = _tile(n, 512, SUBLANES)
    tc = _tile(d, 1024, LANES)
    ncb = d // tc
    half = n // tr
    const = lambda shape: pl.BlockSpec(shape, lambda i, j: (0, 0))
    return pl.pallas_call(
        _filter_body, grid=(2 * n // tr, ncb),
        in_specs=[pl.BlockSpec((tr, zw), lambda i, j: (i, 0)), const((zw, fh)), const((fh, fh)), const((fh, fh)),
                  const((SUBLANES, fh)),
                  pl.BlockSpec((fh, tc), lambda i, j: (0, j + jnp.where(i >= half, ncb, 0))),
                  pl.BlockSpec((1, tc), lambda i, j: (0, j))],
        out_specs=pl.BlockSpec((tr, tc), lambda i, j: (i, j)),
        out_shape=jax.ShapeDtypeStruct((2 * n, d), BF16),
        scratch_shapes=[pltpu.VMEM((tr, fh), F32)], name="hyena_filter",
        compiler_params=_params(("parallel", "arbitrary")),
    )(z, w1p, f_w2.astype(F32), f_w3.astype(F32), bf, f_w4.astype(F32), deltas)


def _dft_tables(n):
    big_n = 2 * n
    n2 = DFT_INNER
    n1 = big_n // n2
    hk = n1 // 2
    k1 = jnp.arange(hk, dtype=jnp.int32)
    m1 = jnp.arange(n1, dtype=jnp.int32)
    ang_a = (math.pi / n1) * ((m1[None, :] * (2 * k1[:, None] + 1)) % (2 * n1)).astype(F32)
    fwd = jnp.concatenate([jnp.cos(ang_a), -jnp.sin(ang_a)], axis=0)
    inv = (2.0 / big_n) * jnp.concatenate([jnp.cos(ang_a[:, :hk]).T, -jnp.sin(ang_a[:, :hk]).T], axis=1)
    k2 = jnp.arange(n2, dtype=jnp.int32)
    m2 = jnp.arange(n2, dtype=jnp.int32)
    freq = 2 * k1[:, None, None] + 1 + 2 * n1 * k2[None, :, None]
    ang_c = (math.pi / big_n) * ((m2[None, None, :] * freq) % (2 * big_n)).astype(F32)
    gr, gi = jnp.cos(ang_c), -jnp.sin(ang_c)
    mid = jnp.concatenate([jnp.concatenate([gr, -gi], axis=2), jnp.concatenate([gi, gr], axis=2)], axis=1)
    mid_t = jnp.swapaxes(mid, 1, 2)
    return fwd, inv, mid, mid_t


def _dft_rows_body(f_ref, x_ref, o_ref):
    o_ref[...] = _dot(f_ref[...], x_ref[...]).astype(o_ref.dtype)


def _dft_rows(f, x2d, tn):
    r, k = f.shape
    c = x2d.shape[1]
    return pl.pallas_call(
        _dft_rows_body, grid=(c // tn,),
        in_specs=[pl.BlockSpec((r, k), lambda j: (0, 0)), pl.BlockSpec((k, tn), lambda j: (0, j))],
        out_specs=pl.BlockSpec((r, tn), lambda j: (0, j)),
        out_shape=jax.ShapeDtypeStruct((r, c), BF16), name="hyena_dft_outer",
        compiler_params=_params(("parallel",)),
    )(f.astype(BF16), x2d)


def _stack_re_im(ref):
    _, _, n2, tc = ref.shape
    return ref[:, 0].reshape(2 * n2, tc)


def _dft_mid_body(m_ref, a_ref, o_ref):
    n2 = a_ref.shape[2]
    s = _dot(m_ref[0], _stack_re_im(a_ref))
    o_ref[0, 0] = s[:n2].astype(o_ref.dtype)
    o_ref[1, 0] = s[n2:].astype(o_ref.dtype)


def _dft_conv_body(m_ref, t_ref, a_ref, g_ref, o_ref):
    n2 = a_ref.shape[2]
    s = _dot(m_ref[0], _stack_re_im(a_ref))
    sr, si = s[:n2], s[n2:]
    gr, gi = g_ref[0, 0].astype(F32), g_ref[1, 0].astype(F32)
    y = jnp.concatenate([sr * gr - si * gi, sr * gi + si * gr], axis=0).astype(BF16)
    b = _dot(t_ref[0], y)
    o_ref[0, 0] = b[:n2].astype(o_ref.dtype)
    o_ref[1, 0] = b[n2:].astype(o_ref.dtype)


def _dft_mid(mid, a4, spec4=None, mid_t=None):
    _, hk, n2, d = a4.shape
    tc = _tile(d, 1024, LANES)
    mspec = pl.BlockSpec((1, 2 * n2, 2 * n2), lambda k, j: (k, 0, 0))
    dspec = pl.BlockSpec((2, 1, n2, tc), lambda k, j: (0, k, 0, j))
    if spec4 is None:
        body, in_specs, operands, name = _dft_mid_body, [mspec, dspec], [mid.astype(BF16), a4], "hyena_dft_inner"
    else:
        body, in_specs = _dft_conv_body, [mspec, mspec, dspec, dspec]
        operands, name = [mid.astype(BF16), mid_t.astype(BF16), a4, spec4], "hyena_dft_inner_conv"
    return pl.pallas_call(
        body, grid=(hk, d // tc), in_specs=in_specs, out_specs=dspec,
        out_shape=jax.ShapeDtypeStruct(a4.shape, BF16), name=name,
        compiler_params=_params(("parallel", "arbitrary")),
    )(*operands)


def _row_tile(m):
    return _tile(m, 1024, 16)


def _short_conv(h, w_in, w_conv, w_out):
    m, d = h.shape
    tm = _row_tile(m)
    tn_in = _tile(d, 256, LANES)
    ncb = d // tn_in
    b, u = _matmul(h, [(w_in, 0), (w_in, ncb), (w_in, 2 * ncb)], [(d, tn_in, BF16), (d, tn_in, BF16)],
                   _ep_gate_pair, tm=tm, tn=tn_in, name="sc_in_proj")
    tn = _tile(d, 512, LANES)
    tmo = _tile(m, 512, HALO_ROWS)
    w8 = jnp.concatenate([w_conv.astype(F32), jnp.zeros((SUBLANES - 3, d), F32)], axis=0)
    main, prev, nxt = _row_halo_specs(tmo, d, m)
    lhs = [(b, *main), (u, *main), (u, *prev), (u, *nxt), (w8, (SUBLANES, d), lambda i, j, k: (0, 0))]
    return _matmul(lhs, [(w_out, 0)], [(d, tn, BF16)], _ep_plain, tm=tmo, tn=tn, name="sc_out_proj",
                   lhs_fn=_lhs_sc_gate, lhs_shape=(m, d))[0]


def _ffn(h2, w_gate_up, w_down):
    m, d = h2.shape
    f = w_down[0].shape[1]
    tm = _row_tile(m)
    tf = _tile(f, 512, LANES)
    a = _matmul(h2, [(w_gate_up, 0), (w_gate_up, f // tf)], [(f, tf, BF16)], _ep_swiglu,
                tm=tm, tn=tf, name="ffn_gate_up")[0]
    tk = f if f <= 4096 else _tile(f, 6144, LANES)
    tn = _tile(d, 512, LANES)
    return _matmul(a, [(w_down, 0)], [(d, tn, BF16)], _ep_plain, tm=tm, tn=tn, tk=tk, name="ffn_down")[0]


def _rope_tables(n, n_ctx, scale):
    rows = n // GRID_W
    row = jnp.repeat(jnp.arange(rows, dtype=F32), GRID_W)
    col = jnp.tile(jnp.arange(GRID_W, dtype=F32), rows)
    axis_dim = ROPE_DIM // 2
    inv = ROPE_BASE ** (-jnp.arange(0, axis_dim, 2, dtype=F32) / axis_dim)
    ang_r, ang_c = row[:, None] * inv, col[:, None] * inv
    cos = jnp.concatenate([jnp.cos(ang_r), jnp.cos(ang_c)], axis=1)
    sin = jnp.concatenate([jnp.sin(ang_r), jnp.sin(ang_c)], axis=1)
    rot = jnp.concatenate([cos, cos, -sin, sin], axis=1)
    tab_q = scale * jnp.concatenate([jnp.ones((n, NOPE_DIM), F32), rot], axis=1)
    ctx_rot = jnp.concatenate([jnp.ones((n_ctx, ROPE_DIM), F32), jnp.zeros((n_ctx, ROPE_DIM), F32)], axis=1)
    return tab_q, rot, ctx_rot


def _mla(h, hc, w_down, q_norm, kv_norm, w_uq, w_ukv, w_out):
    n, d = h.shape
    n_ctx = hc.shape[0]
    q_rank, kv_rank = w_uq.shape[0], w_ukv.shape[0]
    heads = w_uq.shape[1] // (NOPE_DIM + ROPE_DIM)
    scale = (NOPE_DIM + ROPE_DIM) ** -0.5 * math.log2(math.e)
    tab_q, tab_k, tab_kc = _rope_tables(n, n_ctx, scale)

    def rope_cols(w):
        qd = ROPE_DIM // 4
        a = jnp.concatenate([w[..., 0:qd], w[..., 2 * qd:3 * qd]], axis=-1)
        b = jnp.concatenate([w[..., qd:2 * qd], w[..., 3 * qd:4 * qd]], axis=-1)
        return jnp.concatenate([a, b, b, a], axis=-1)

    w_dq = w_down[:, :q_rank].astype(BF16)
    w_dkv = jnp.concatenate([w_down[:, q_rank:q_rank + kv_rank], rope_cols(w_down[:, q_rank + kv_rank:])],
                            axis=1).astype(BF16)
    wq3 = w_uq.reshape(q_rank, heads, NOPE_DIM + ROPE_DIM)
    w_uq_p = jnp.concatenate([wq3[:, :, :NOPE_DIM], rope_cols(wq3[:, :, NOPE_DIM:])], axis=2)
    qw = NOPE_DIM + 2 * ROPE_DIM
    w_uq_p = w_uq_p.reshape(q_rank, heads * qw).astype(BF16)
    wkv3 = w_ukv.reshape(kv_rank, heads, NOPE_DIM + V_DIM)
    w_kn = wkv3[:, :, :NOPE_DIM].reshape(kv_rank, heads * NOPE_DIM).astype(BF16)
    w_vt = wkv3[:, :, NOPE_DIM:].reshape(kv_rank, heads * V_DIM).T.astype(BF16)

    tm = _row_tile(n)
    cqn = _matmul(h, [(w_dq, 0)], [(q_rank, q_rank, BF16)], _ep_rms,
                  extras=[(q_norm.reshape(1, q_rank).astype(F32), (1, q_rank), lambda i, j, k: (0, 0))],
                  tm=tm, tn=q_rank, name="mla_q_down")[0]

    def kv_down(hh, tab, name):
        m = hh.shape[0]
        tmk = _row_tile(m)
        wd = kv_rank + 2 * ROPE_DIM
        return _matmul(hh, [(w_dkv, 0)], [(kv_rank, kv_rank, BF16), (LANES, LANES, BF16)],
                       functools.partial(_ep_kv_down, kv_rank=kv_rank),
                       extras=[(kv_norm.reshape(1, kv_rank).astype(F32), (1, kv_rank), lambda i, j, k: (0, 0)),
                               (tab, (tmk, 2 * ROPE_DIM), lambda i, j, k: (i, 0))],
                       tm=tmk, tn=wd, name=name)

    ckv, kr = kv_down(h, tab_k, "mla_kv_down")
    ckv_c, kr_c = kv_down(hc, tab_kc, "mla_kv_down_ctx")
    ckv = jnp.concatenate([ckv, ckv_c], axis=0)
    kr = jnp.concatenate([kr, kr_c], axis=0)
    nk = n + n_ctx

    gq = 2 if heads % 2 == 0 else 1
    q = _matmul(cqn, [(w_uq_p, 0)], [(heads * qw, gq * qw, BF16)], _ep_q_up,
                extras=[(tab_q, (tm, qw), lambda i, j, k: (i, 0))], tm=tm, tn=gq * qw, name="mla_q_up")[0]
    tmk = _tile(nk, 1664, 16)
    k = _matmul(ckv, [(w_kn, 0)], [(heads * qw, gq * qw, BF16)], _ep_k_up,
                extras=[(kr, (tmk, LANES), lambda i, j, k: (i, 0))], tm=tmk, tn=gq * NOPE_DIM, name="mla_k_up")[0]
    vt = _v_up_transposed(w_vt, ckv)
    o = _attention(q, k, vt, heads)
    tn = _tile(d, 512, LANES)
    return _matmul(o, [(w_out, 0)], [(d, tn, BF16)], _ep_plain, tm=tm, tn=tn, name="mla_out_proj")[0]


def _hyena(h, w_in, conv_w, conv_b, f_w1, f_b1, f_w2, f_b2, f_w3, f_b3, f_freq, f_w4, skip, w_out):
    n, d = h.shape
    tm = _row_tile(n)
    tn = _tile(d, 512, LANES)
    z = _matmul(h, [(w_in, 0)], [(3 * d, tn, BF16)], _ep_plain, tm=tm, tn=tn, name="hy_in_proj")[0]
    x0, vv = _hy_gate(z, conv_w, conv_b)
    n2 = DFT_INNER
    n1 = 2 * n // n2
    hk = n1 // 2
    fwd, inv, mid, mid_t = _dft_tables(n)
    g = _hyena_filter(n, d, f_w1, f_b1, f_w2, f_b2, f_w3, f_b3, f_freq, f_w4)
    tcol = _tile(n2 * d, 4096, LANES)
    ga = _dft_rows(fwd, g.reshape(n1, n2 * d), tcol)
    spec = _dft_mid(mid, ga.reshape(2, hk, n2, d))
    va = _dft_rows(fwd[:, :hk], vv.reshape(hk, n2 * d), tcol)
    vb = _dft_mid(mid, va.reshape(2, hk, n2, d), spec, mid_t)
    y = _dft_rows(inv, vb.reshape(n1, n2 * d), tcol).reshape(n, d)
    tmo = _tile(n, 512, 16)
    row = ((tmo, d), lambda i, j, k: (i, 0))
    lhs = [(x0, *row), (y, *row), (vv, *row), (skip.reshape(1, d).astype(F32), (1, d), lambda i, j, k: (0, 0))]
    return _matmul(lhs, [(w_out, 0)], [(d, tn, BF16)], _ep_plain, tm=tmo, tn=tn, name="hy_out_proj",
                   lhs_fn=_lhs_hy_gate, lhs_shape=(n, d))[0]


def kernel(x, c, ctx, c_ctx, ada_down, ada_up, ada_bias, norm_gain, ffn_w_gate_up, ffn_w_down, sc_w_in, sc_conv, sc_w_out, mla_w_down, mla_q_norm, mla_kv_norm, mla_w_uq, mla_w_ukv, mla_w_out, hy_w_in, hy_conv, hy_conv_b, hy_f_w1, hy_f_b1, hy_f_w2, hy_f_b2, hy_f_w3, hy_f_b3, hy_f_freq, hy_f_w4, hy_skip, hy_w_out):
    batch, n, d = x.shape
    assert batch == 1 and c.shape[0] == 1 and ctx.shape[0] == 1
    depth = ada_down.shape[0]
    n_mixers = 3
    xs = x.reshape(n, d)
    cs = ctx.reshape(ctx.shape[1], d)

    mla_layers = [i for i in range(depth) if i % n_mixers == 1]
    last_ctx_read = mla_layers[-1] if mla_layers else -1

    s_raw = jnp.concatenate([c.reshape(1, d), c_ctx.reshape(1, d), jnp.zeros((2 * SUBLANES - 2, d), F32)], axis=0)
    mods = _adaln(s_raw, ada_down, ada_up, ada_bias)

    def mod_vecs(i, row):
        return [mods[i, row, m * d:(m + 1) * d] for m in range(N_MOD)]

    ffn_gu, ffn_dn = ffn_w_gate_up, ffn_w_down.astype(BF16)
    sc_in, sc_out = sc_w_in.astype(BF16), sc_w_out.astype(BF16)
    hy_in, hy_out = hy_w_in.astype(BF16), hy_w_out.astype(BF16)
    mla_out = mla_w_out.astype(BF16)
    pend = None
    pend_c = None
    for i in range(depth):
        kind, j = i % n_mixers, i // n_mixers
        ctx_full = i < last_ctx_read
        ctx_keys = i == last_ctx_read
        g = norm_gain[i]
        streams = [(0, xs, pend)]
        if ctx_full or ctx_keys:
            streams.append((1, cs, pend_c))
        hs = {}
        cur = {}
        for row, xv, pd in streams:
            mv = mod_vecs(i, row)
            if pd is None:
                _, hh = _resid_norm_mod(xv, None, _vec_rows(d, g[0], g[0], g[0], mv[0], mv[1]), has_h=True)
            else:
                xv, hh = _resid_norm_mod(xv, pd[0], _vec_rows(d, pd[1], pd[2], g[0], mv[0], mv[1]), has_h=True)
            hs[row], cur[row] = hh, xv

        ys = {}
        if kind == 0:
            w_in, w_out = (sc_in, j), (sc_out, j)
            ys[0] = _short_conv(hs[0], w_in, sc_conv[j], w_out)
            if ctx_full:
                ys[1] = _short_conv(hs[1], w_in, sc_conv[j], w_out)
        elif kind == 1:
            ys[0] = _mla(hs[0], hs[1], mla_w_down[j], mla_q_norm[j], mla_kv_norm[j], mla_w_uq[j], mla_w_ukv[j],
                         (mla_out, j))
            assert not ctx_full
        else:
            hp = ((hy_in, j), hy_conv[j], hy_conv_b[j], hy_f_w1[j], hy_f_b1[j], hy_f_w2[j], hy_f_b2[j],
                  hy_f_w3[j], hy_f_b3[j], hy_f_freq[j], hy_f_w4[j], hy_skip[j], (hy_out, j))
            ys[0] = _hyena(hs[0], *hp)
            if ctx_full:
                ys[1] = _hyena(hs[1], *hp)

        w_gu, w_dn = (ffn_gu, i), (ffn_dn, i)
        new_pend = {0: None, 1: None}
        for row in ys:
            mv = mod_vecs(i, row)
            xv, h2 = _resid_norm_mod(cur[row], ys[row], _vec_rows(d, mv[2], g[1], g[2], mv[3], mv[4]), has_h=True)
            cur[row] = xv
            new_pend[row] = (_ffn(h2, w_gu, w_dn), mv[5], g[3])
        xs, pend = cur[0], new_pend[0]
        if ctx_full:
            cs, pend_c = cur[1], new_pend[1]
        else:
            pend_c = None

    xs, _ = _resid_norm_mod(xs, pend[0], _vec_rows(d, pend[1], pend[2], pend[2], pend[1], pend[1]), has_h=False)
    return xs.reshape(batch, n, d)
```

```python
import functools
import math

import jax
import jax.numpy as jnp
from jax import lax
from jax.experimental import pallas as pl
from jax.experimental.pallas import tpu as pltpu

F32 = jnp.float32
BF16 = jnp.bfloat16

EPS = 1e-6
N_MOD = 6
NOPE_DIM = 128
ROPE_DIM = 64
V_DIM = 128
GRID_W = 64
ROPE_BASE = 10000.0
HY_BANDS = 16
HY_DECAY_TARGET = 1e-2
HY_FAST_DECAY_PCT = 0.3
HY_SLOW_DECAY_PCT = 1.5

LANES = 128
SUBLANES = 8
VMEM_LIMIT_BYTES = 56 * 1024 * 1024
DFT_INNER = 128
HALO_ROWS = 16
MXU_DEPTH = 256

def _tile(dim, pref, align):
    best = None
    t = align
    while t <= min(dim, pref):
        if dim % t == 0:
            best = t
        t += align
    return best if best is not None else dim


def _params(sem):
    return pltpu.CompilerParams(dimension_semantics=sem, vmem_limit_bytes=VMEM_LIMIT_BYTES)


def _split_hi_lo(x):
    hi = x.astype(BF16)
    lo = (x - hi.astype(F32)).astype(BF16)
    return hi, lo


def _dot(a, b):
    return jnp.dot(a, b, preferred_element_type=F32)


def _dot3(a, b):
    ah, al = _split_hi_lo(a)
    bh, bl = _split_hi_lo(b)
    return _dot(ah, bh) + _dot(ah, bl) + _dot(al, bh)


def _rms(x, gain):
    return x * lax.rsqrt(jnp.mean(x * x, axis=-1, keepdims=True) + EPS) * gain


def _silu(x):
    return x * (1.0 / (1.0 + jnp.exp(-x)))


def _mm_body(*refs, n_lhs, n_w, n_extra, n_out, nk, epilogue, lhs_fn):
    refs = list(refs)
    lhs_refs = [refs.pop(0) for _ in range(n_lhs)]
    w_refs = [refs.pop(0) for _ in range(n_w)]
    extra_refs = [refs.pop(0) for _ in range(n_extra)]
    out_refs = [refs.pop(0) for _ in range(n_out)]
    acc_refs = [refs.pop(0) for _ in range(n_w if nk > 1 else 0)]
    if lhs_fn is None:
        a = lhs_refs[0][...].astype(BF16)
    else:
        lhs_scratch = refs.pop(0)

        @pl.when(pl.program_id(1) == 0)
        def _():
            lhs_fn(lhs_refs, lhs_scratch)

        a = lhs_scratch[...]
    dots = [_dot(a, w[...].astype(BF16)) for w in w_refs]
    if nk == 1:
        epilogue(dots, extra_refs, out_refs)
        return
    k = pl.program_id(2)

    @pl.when(k == 0)
    def _():
        for acc, d in zip(acc_refs, dots):
            acc[...] = d

    @pl.when(k > 0)
    def _():
        for acc, d in zip(acc_refs, dots):
            acc[...] += d

    @pl.when(k == nk - 1)
    def _():
        epilogue([acc[...] for acc in acc_refs], extra_refs, out_refs)


def _matmul(a, ws, outs, epilogue, *, extras=(), tm, tn, tk=None, name, lhs_fn=None, lhs_shape=None):
    if lhs_fn is None:
        M, K = a.shape
    else:
        M, K = lhs_shape
    tk = K if tk is None else tk
    nk = K // tk
    assert lhs_fn is None or nk == 1
    n_col_blocks = outs[0][0] // outs[0][1]
    grid = (M // tm, n_col_blocks, nk)
    if lhs_fn is None:
        in_specs = [pl.BlockSpec((tm, tk), lambda i, j, k: (i, k))]
        operands = [a]
    else:
        in_specs = [pl.BlockSpec(bshape, imap) for _, bshape, imap in a]
        operands = [arr for arr, _, _ in a]
    n_lhs = len(operands)
    for w, off in ws:
        if isinstance(w, tuple):
            w, layer = w
            in_specs.append(pl.BlockSpec((None, tk, tn),
                                         functools.partial(lambda i, j, k, o, l: (l, k, j + o), o=off, l=layer)))
        else:
            in_specs.append(pl.BlockSpec((tk, tn), functools.partial(lambda i, j, k, o: (k, j + o), o=off)))
        operands.append(w)
    for arr, bshape, imap in extras:
        in_specs.append(pl.BlockSpec(bshape, imap))
        operands.append(arr)
    out_shape = [jax.ShapeDtypeStruct((M, wt), dt) for wt, _, dt in outs]
    out_specs = [pl.BlockSpec((tm, bw), lambda i, j, k: (i, j)) for _, bw, _ in outs]
    scratch = [pltpu.VMEM((tm, tn), F32) for _ in ws] if nk > 1 else []
    if lhs_fn is not None:
        scratch.append(pltpu.VMEM((tm, K), BF16))
    body = functools.partial(_mm_body, n_lhs=n_lhs, n_w=len(ws), n_extra=len(extras), n_out=len(outs), nk=nk,
                             epilogue=epilogue, lhs_fn=lhs_fn)
    res = pl.pallas_call(
        body, grid=grid, in_specs=in_specs, out_specs=out_specs, out_shape=out_shape,
        scratch_shapes=scratch, name=name,
        compiler_params=_params(("parallel", "arbitrary", "arbitrary")),
    )(*operands)
    return res


def _ep_plain(dots, extras, outs):
    outs[0][...] = dots[0].astype(outs[0].dtype)


def _ep_gate_pair(dots, extras, outs):
    outs[0][...] = dots[0].astype(outs[0].dtype)
    outs[1][...] = (dots[1] * dots[2]).astype(outs[1].dtype)


def _ep_swiglu(dots, extras, outs):
    outs[0][...] = (_silu(dots[0]) * dots[1]).astype(outs[0].dtype)


def _ep_rms(dots, extras, outs):
    outs[0][...] = _rms(dots[0], extras[0][...]).astype(outs[0].dtype)


def _ep_kv_down(dots, extras, outs, *, kv_rank):
    gain_ref, tab_ref = extras
    d = dots[0]
    outs[0][...] = _rms(d[:, :kv_rank], gain_ref[...]).astype(outs[0].dtype)
    p = d[:, kv_rank:] * tab_ref[...]
    outs[1][...] = (p + pltpu.roll(p, ROPE_DIM, axis=1)).astype(outs[1].dtype)


def _ep_k_up(dots, extras, outs):
    kn = dots[0]
    kr = extras[0][...]
    for g in range(kn.shape[1] // NOPE_DIM):
        base = g * (NOPE_DIM + LANES)
        outs[0][:, base:base + NOPE_DIM] = kn[:, g * NOPE_DIM:(g + 1) * NOPE_DIM].astype(outs[0].dtype)
        outs[0][:, base + NOPE_DIM:base + NOPE_DIM + LANES] = kr


def _adaln_body(s_ref, down_ref, up_ref, bias_ref, out_ref, t_ref):
    @pl.when(pl.program_id(1) == 0)
    def _():
        t_ref[...] = _dot3(_silu(s_ref[...]), down_ref[0])

    out_ref[0] = _dot3(t_ref[...], up_ref[0]) + bias_ref[0]


def _adaln(s_raw, ada_down, ada_up, ada_bias):
    depth, d, r = ada_down.shape
    n6 = ada_up.shape[2]
    rows = s_raw.shape[0]
    tn = _tile(n6, 2048, LANES)
    return pl.pallas_call(
        _adaln_body, grid=(depth, n6 // tn),
        in_specs=[pl.BlockSpec((rows, d), lambda i, j: (0, 0)),
                  pl.BlockSpec((1, d, r), lambda i, j: (i, 0, 0)),
                  pl.BlockSpec((1, r, tn), lambda i, j: (i, 0, j)),
                  pl.BlockSpec((1, 1, tn), lambda i, j: (i, 0, j))],
        out_specs=pl.BlockSpec((1, rows, tn), lambda i, j: (i, 0, j)),
        out_shape=jax.ShapeDtypeStruct((depth, rows, n6), F32),
        scratch_shapes=[pltpu.VMEM((rows, r), F32)], name="adaln",
        compiler_params=_params(("arbitrary", "arbitrary")),
    )(s_raw, ada_down, ada_up, ada_bias.reshape(depth, 1, n6))


def _rnm_body(*refs, has_resid, has_h):
    refs = list(refs)
    x_ref = refs.pop(0)
    y_ref = refs.pop(0) if has_resid else None
    vec_ref = refs.pop(0)
    x = x_ref[...]
    if has_resid:
        xo_ref = refs.pop(0)
        x = x + vec_ref[0:1, :] * _rms(y_ref[...].astype(F32), vec_ref[1:2, :])
        xo_ref[...] = x
    if has_h:
        h_ref = refs.pop(0)
        h_ref[...] = (_rms(x, vec_ref[2:3, :]) * (1.0 + vec_ref[4:5, :]) + vec_ref[3:4, :]).astype(h_ref.dtype)


def _resid_norm_mod(x, y, vec, *, has_h, rows=None):
    n, d = x.shape
    n = n if rows is None else rows
    has_resid = y is not None
    tr = _tile(n, 256, 16)
    spec = pl.BlockSpec((tr, d), lambda i: (i, 0))
    in_specs = [spec] + ([spec] if has_resid else []) + [pl.BlockSpec(vec.shape, lambda i: (0, 0))]
    operands = [x] + ([y] if has_resid else []) + [vec]
    out_shape, out_specs = [], []
    if has_resid:
        out_shape.append(jax.ShapeDtypeStruct((n, d), F32))
        out_specs.append(spec)
    if has_h:
        out_shape.append(jax.ShapeDtypeStruct((n, d), BF16))
        out_specs.append(spec)
    res = pl.pallas_call(
        functools.partial(_rnm_body, has_resid=has_resid, has_h=has_h), grid=(n // tr,),
        in_specs=in_specs, out_specs=out_specs, out_shape=out_shape, name="resid_norm_mod",
        compiler_params=_params(("parallel",)),
    )(*operands)
    res = list(res)
    x_new = res.pop(0) if has_resid else None
    h = res.pop(0) if has_h else None
    return x_new, h


def _vec_rows(d, *rows):
    out = [r.reshape(1, d).astype(F32) for r in rows]
    out += [jnp.zeros((1, d), F32)] * (SUBLANES - len(out))
    return jnp.concatenate(out, axis=0)


def _conv3_rows(main, prev_blk, next_blk, w0, w1, w2, is_first, is_last):
    main = main.astype(F32)
    tr = main.shape[0]
    rows = lax.broadcasted_iota(jnp.int32, main.shape, 0)
    prev_row = jnp.where(is_first, 0.0, prev_blk[HALO_ROWS - 1:HALO_ROWS, :].astype(F32))
    next_row = jnp.where(is_last, 0.0, next_blk[0:1, :].astype(F32))
    up = jnp.where(rows == 0, prev_row, pltpu.roll(main, 1, axis=0))
    dn = jnp.where(rows == tr - 1, next_row, pltpu.roll(main, tr - 1, axis=0))
    return w0 * up + w1 * main + w2 * dn


def _halo_specs(tr, tc, n_rows, col_off):
    per = tr // HALO_ROWS
    last = n_rows // HALO_ROWS - 1
    return [
        pl.BlockSpec((tr, tc), lambda i, j: (i, j + col_off)),
        pl.BlockSpec((HALO_ROWS, tc), lambda i, j: (jnp.maximum(i * per - 1, 0), j + col_off)),
        pl.BlockSpec((HALO_ROWS, tc), lambda i, j: (jnp.minimum((i + 1) * per, last), j + col_off)),
    ]


LHS_CHUNK = 512


def _lhs_sc_gate(refs, scratch):
    b_ref, u_ref, up_ref, un_ref, w_ref = refs
    i = pl.program_id(0)
    first, last = i == 0, i == pl.num_programs(0) - 1
    width = scratch.shape[1]
    cw = min(width, LHS_CHUNK)
    for c in range(0, width, cw):
        sl = slice(c, c + cw)
        conv = _conv3_rows(u_ref[:, sl], up_ref[:, sl], un_ref[:, sl], w_ref[0:1, sl], w_ref[1:2, sl],
                           w_ref[2:3, sl], first, last)
        scratch[:, sl] = (b_ref[:, sl].astype(F32) * conv).astype(scratch.dtype)


def _lhs_hy_gate(refs, scratch):
    x0_ref, y_ref, v_ref, skip_ref = refs
    width = scratch.shape[1]
    cw = min(width, LHS_CHUNK)
    for c in range(0, width, cw):
        sl = slice(c, c + cw)
        yv = y_ref[:, sl] + v_ref[:, sl] * skip_ref[0:1, sl]
        scratch[:, sl] = x0_ref[:, sl] * yv


def _row_halo_specs(tm, d, n_rows):
    per = tm // HALO_ROWS
    last = n_rows // HALO_ROWS - 1
    return [
        ((tm, d), lambda i, j, k: (i, 0)),
        ((HALO_ROWS, d), lambda i, j, k: (jnp.maximum(i * per - 1, 0), 0)),
        ((HALO_ROWS, d), lambda i, j, k: (jnp.minimum((i + 1) * per, last), 0)),
    ]


def _hy_gate_body(*refs):
    z_refs = refs[0:9]
    w_refs = refs[9:12]
    x0_ref, vv_ref = refs[12:14]
    i = pl.program_id(0)
    first, last = i == 0, i == pl.num_programs(0) - 1
    conv = []
    for g in range(3):
        m, p, nx = z_refs[3 * g:3 * g + 3]
        w = w_refs[g]
        conv.append(_conv3_rows(m[...], p[...], nx[...], w[0:1, :], w[1:2, :], w[2:3, :], first, last) + w[3:4, :])
    x0_ref[...] = conv[0].astype(x0_ref.dtype)
    vv_ref[...] = (conv[1] * conv[2]).astype(vv_ref.dtype)


def _hy_gate(z, conv_w, conv_b):
    n, d3 = z.shape
    d = d3 // 3
    tr = _tile(n, 512, 16)
    tc = _tile(d, 512, LANES)
    ncb = d // tc
    w8 = jnp.concatenate([conv_w.astype(F32), conv_b.reshape(1, d3).astype(F32),
                          jnp.zeros((SUBLANES - 4, d3), F32)], axis=0)
    in_specs, operands = [], []
    for g in range(3):
        in_specs += _halo_specs(tr, tc, n, g * ncb)
        operands += [z, z, z]
    for g in range(3):
        in_specs.append(pl.BlockSpec((SUBLANES, tc), functools.partial(lambda i, j, o: (0, j + o), o=g * ncb)))
        operands.append(w8)
    spec = pl.BlockSpec((tr, tc), lambda i, j: (i, j))
    return pl.pallas_call(
        _hy_gate_body, grid=(n // tr, ncb), in_specs=in_specs, out_specs=[spec, spec],
        out_shape=[jax.ShapeDtypeStruct((n, d), BF16)] * 2, name="hy_conv_gate",
        compiler_params=_params(("parallel", "parallel")),
    )(*operands)


def _attn_body(q_ref, k_ref, vt_ref, o_ref, m_ref, l_ref, acc_ref, *, ck):
    ki = pl.program_id(2)

    @pl.when(ki == 0)
    def _():
        m_ref[...] = jnp.full(m_ref.shape, -jnp.inf, F32)
        l_ref[...] = jnp.zeros(l_ref.shape, F32)
        acc_ref[...] = jnp.zeros(acc_ref.shape, F32)

    q = q_ref[...]
    m, l, acc = m_ref[...], l_ref[...], acc_ref[...]
    tk = k_ref.shape[0]
    bounds = [(lo, min(lo + ck, tk)) for lo in range(0, tk, ck)]

    def scores(b):
        return _dot(k_ref[b[0]:b[1], :], q)

    s_next = scores(bounds[0])
    pending = None
    for c, b in enumerate(bounds):
        s = s_next
        if c + 1 < len(bounds):
            s_next = scores(bounds[c + 1])
        if pending is not None:
            a_prev, p_prev, b_prev = pending
            acc = a_prev * acc + _dot(vt_ref[:, b_prev[0]:b_prev[1]], p_prev)
        m_new = jnp.maximum(m, jnp.max(s, axis=0, keepdims=True))
        alpha = jnp.exp2(m - m_new)
        p = jnp.exp2(s - m_new)
        l = alpha * l + jnp.sum(p, axis=0, keepdims=True)
        pending = (alpha, p.astype(BF16), b)
        m = m_new
    a_prev, p_prev, b_prev = pending
    acc = a_prev * acc + _dot(vt_ref[:, b_prev[0]:b_prev[1]], p_prev)
    m_ref[...], l_ref[...], acc_ref[...] = m, l, acc

    @pl.when(ki == pl.num_programs(2) - 1)
    def _():
        o_ref[...] = (acc * (1.0 / l)).T.astype(o_ref.dtype)


def _attention(q, k, vt, heads):
    n = q.shape[1]
    nk = k.shape[0]
    qw = q.shape[0] // heads
    unit = MXU_DEPTH if nk % MXU_DEPTH == 0 else LANES
    ck = 3 * MXU_DEPTH
    tq = _tile(n, 2048, LANES)
    tk = _tile(nk, 13 * MXU_DEPTH, unit)
    return pl.pallas_call(
        functools.partial(_attn_body, ck=ck), grid=(heads, n // tq, nk // tk),
        in_specs=[pl.BlockSpec((qw, tq), lambda h, i, j: (h, i)),
                  pl.BlockSpec((tk, qw), lambda h, i, j: (j, h)),
                  pl.BlockSpec((V_DIM, tk), lambda h, i, j: (h, j))],
        out_specs=pl.BlockSpec((tq, V_DIM), lambda h, i, j: (i, h)),
        out_shape=jax.ShapeDtypeStruct((n, heads * V_DIM), BF16),
        scratch_shapes=[pltpu.VMEM((1, tq), F32), pltpu.VMEM((1, tq), F32), pltpu.VMEM((V_DIM, tq), F32)],
        name="mla_flash_attention",
        compiler_params=_params(("parallel", "parallel", "arbitrary")),
    )(q, k, vt)


def _qt_body(w_ref, c_ref, tab_ref, o_ref):
    x = lax.dot_general(w_ref[...], c_ref[...], (((1,), (1,)), ((), ())), preferred_element_type=F32)
    tab = tab_ref[...]
    width = tab.shape[0]
    for g in range(x.shape[0] // width):
        o_ref[g * width:(g + 1) * width, :] = (x[g * width:(g + 1) * width] * tab).astype(o_ref.dtype)


def _q_up_transposed(w_qt, cqn, tab_t):
    hq, r = w_qt.shape
    n = cqn.shape[0]
    qw = tab_t.shape[0]
    th = _tile(hq, 2 * qw, qw)
    tn = _tile(n, 1024, LANES)
    return pl.pallas_call(
        _qt_body, grid=(hq // th, n // tn),
        in_specs=[pl.BlockSpec((th, r), lambda i, j: (i, 0)), pl.BlockSpec((tn, r), lambda i, j: (j, 0)),
                  pl.BlockSpec((qw, tn), lambda i, j: (0, j))],
        out_specs=pl.BlockSpec((th, tn), lambda i, j: (i, j)),
        out_shape=jax.ShapeDtypeStruct((hq, n), BF16), name="mla_q_up_transposed",
        compiler_params=_params(("parallel", "parallel")),
    )(w_qt, cqn, tab_t)


def _vt_body(w_ref, c_ref, o_ref):
    o_ref[...] = lax.dot_general(w_ref[...], c_ref[...], (((1,), (1,)), ((), ())),
                                 preferred_element_type=F32).astype(o_ref.dtype)


def _v_up_transposed(w_vt, ckv):
    hv, r = w_vt.shape
    nk = ckv.shape[0]
    th = _tile(hv, 2 * V_DIM, V_DIM)
    tn = _tile(nk, 13 * MXU_DEPTH, LANES)
    return pl.pallas_call(
        _vt_body, grid=(hv // th, nk // tn),
        in_specs=[pl.BlockSpec((th, r), lambda i, j: (i, 0)), pl.BlockSpec((tn, r), lambda i, j: (j, 0))],
        out_specs=pl.BlockSpec((th, tn), lambda i, j: (i, j)),
        out_shape=jax.ShapeDtypeStruct((hv, nk), BF16), name="mla_v_up_transposed",
        compiler_params=_params(("parallel", "parallel")),
    )(w_vt, ckv)


def _filter_body(z_ref, w1_ref, w2_ref, w3_ref, bf_ref, w4f_ref, w4b_ref, dl_ref, o_ref, h_ref):
    @pl.when(pl.program_id(1) == 0)
    def _():
        bf = bf_ref[...]
        h = jnp.sin(bf[3:4, :] * (_dot3(z_ref[0], w1_ref[...]) + bf[0:1, :]))
        h = jnp.sin(bf[4:5, :] * (_dot3(h, w2_ref[...]) + bf[1:2, :]))
        h_ref[...] = jnp.sin(bf[5:6, :] * (_dot3(h, w3_ref[...]) + bf[2:3, :]))

    z = z_ref[0]
    t = z[:, 0:1]
    sign = z[:, HY_BANDS * 2 + 1:HY_BANDS * 2 + 2]
    h = h_ref[...].astype(BF16)
    half = h.shape[0] // 2
    filt = jnp.concatenate([_dot(h[:half], w4f_ref[...].astype(BF16)), _dot(h[half:], w4b_ref[...].astype(BF16))],
                           axis=0)
    o_ref[...] = (sign * filt * jnp.exp(-t * dl_ref[...])).astype(o_ref.dtype)


def _hyena_filter(n, d, f_w1, f_b1, f_w2, f_b2, f_w3, f_b3, f_freq, f_w4):
    fh = f_w1.shape[1]
    emb = f_w1.shape[0]
    r = jnp.arange(2 * n, dtype=jnp.int32)
    p = jnp.minimum(jnp.where(r < n, r, 2 * n - r), n - 1)
    pf = p.astype(F32)[:, None]
    t = pf / (n - 1)
    w = (2.0 * math.pi / n) * pf
    bands = jnp.linspace(1e-4, HY_BANDS - 1, HY_BANDS, dtype=F32)
    sign = jnp.where(r < n, 1.0, jnp.where(r == n, 0.0, -1.0)).astype(F32)[:, None]
    zw = LANES // 2
    z = jnp.concatenate([t, jnp.cos(bands * w), -jnp.sin(bands * w), sign,
                         jnp.zeros((2 * n, zw - emb - 1), F32)], axis=-1)
    w1p = jnp.concatenate([f_w1.astype(F32), jnp.zeros((zw - emb, fh), F32)], axis=0)
    bf = jnp.concatenate([f_b1.reshape(1, fh), f_b2.reshape(1, fh), f_b3.reshape(1, fh),
                          f_freq.reshape(3, fh), jnp.zeros((2, fh), F32)], axis=0).astype(F32)
    deltas = jnp.abs(jnp.linspace(math.log(HY_DECAY_TARGET) / HY_SLOW_DECAY_PCT,
                                  math.log(HY_DECAY_TARGET) / HY_FAST_DECAY_PCT, d, dtype=F32)).reshape(1, d)
    n2 = DFT_INNER
    n1 = 2 * n // n2
    zt = z.reshape(n1, n2, zw).transpose(1, 0, 2)
    tc = _tile(d, 2048, LANES)
    ncb = d // tc
    const = lambda shape: pl.BlockSpec(shape, lambda i, j: (0, 0))
    w4 = f_w4.astype(F32)
    return pl.pallas_call(
        _filter_body, grid=(n2, ncb),
        in_specs=[pl.BlockSpec((1, n1, zw), lambda i, j: (i, 0, 0)), const((zw, fh)), const((fh, fh)),
                  const((fh, fh)), const((SUBLANES, fh)),
                  pl.BlockSpec((fh, tc), lambda i, j: (0, j)),
                  pl.BlockSpec((fh, tc), lambda i, j: (0, j + ncb)),
                  pl.BlockSpec((1, tc), lambda i, j: (0, j))],
        out_specs=pl.BlockSpec((n1, tc), lambda i, j: (0, i * ncb + j)),
        out_shape=jax.ShapeDtypeStruct((n1, n2 * d), BF16),
        scratch_shapes=[pltpu.VMEM((n1, fh), F32)], name="hyena_filter",
        compiler_params=_params(("parallel", "arbitrary")),
    )(zt, w1p, f_w2.astype(F32), f_w3.astype(F32), bf, w4, w4, deltas)


def _dft_tables(n):
    big_n = 2 * n
    n2 = DFT_INNER
    n1 = big_n // n2
    hk = n1 // 2
    k1 = jnp.arange(hk, dtype=jnp.int32)
    m1 = jnp.arange(n1, dtype=jnp.int32)
    ang_a = (math.pi / n1) * ((m1[None, :] * (2 * k1[:, None] + 1)) % (2 * n1)).astype(F32)
    fwd = jnp.concatenate([jnp.cos(ang_a), -jnp.sin(ang_a)], axis=0)
    inv = (2.0 / big_n) * jnp.concatenate([jnp.cos(ang_a[:, :hk]).T, -jnp.sin(ang_a[:, :hk]).T], axis=1)
    k2 = jnp.arange(n2, dtype=jnp.int32)
    m2 = jnp.arange(n2, dtype=jnp.int32)
    freq = 2 * k1[:, None, None] + 1 + 2 * n1 * k2[None, :, None]
    ang_c = (math.pi / big_n) * ((m2[None, None, :] * freq) % (2 * big_n)).astype(F32)
    gr, gi = jnp.cos(ang_c), -jnp.sin(ang_c)
    mid = jnp.concatenate([jnp.concatenate([gr, -gi], axis=2), jnp.concatenate([gi, gr], axis=2)], axis=1)
    mid_t = jnp.swapaxes(mid, 1, 2)
    return fwd, inv, mid, mid_t


def _dft_rows_body(f_ref, x_ref, o_ref):
    o_ref[...] = _dot(f_ref[...], x_ref[...]).astype(o_ref.dtype)


def _dft_rows(f, x2d, tn):
    r, k = f.shape
    c = x2d.shape[1]
    return pl.pallas_call(
        _dft_rows_body, grid=(c // tn,),
        in_specs=[pl.BlockSpec((r, k), lambda j: (0, 0)), pl.BlockSpec((k, tn), lambda j: (0, j))],
        out_specs=pl.BlockSpec((r, tn), lambda j: (0, j)),
        out_shape=jax.ShapeDtypeStruct((r, c), BF16), name="hyena_dft_outer",
        compiler_params=_params(("parallel",)),
    )(f.astype(BF16), x2d)


def _stack_re_im(ref):
    _, _, n2, tc = ref.shape
    return ref[:, 0].reshape(2 * n2, tc)


def _dft_mid_body(m_ref, a_ref, o_ref):
    n2 = a_ref.shape[2]
    s = _dot(m_ref[0], _stack_re_im(a_ref))
    o_ref[0, 0] = s[:n2].astype(o_ref.dtype)
    o_ref[1, 0] = s[n2:].astype(o_ref.dtype)


def _dft_conv_body(m_ref, t_ref, a_ref, g_ref, o_ref):
    n2 = a_ref.shape[2]
    s = _dot(m_ref[0], _stack_re_im(a_ref))
    sr, si = s[:n2], s[n2:]
    gr, gi = g_ref[0, 0].astype(F32), g_ref[1, 0].astype(F32)
    y = jnp.concatenate([sr * gr - si * gi, sr * gi + si * gr], axis=0).astype(BF16)
    b = _dot(t_ref[0], y)
    o_ref[0, 0] = b[:n2].astype(o_ref.dtype)
    o_ref[1, 0] = b[n2:].astype(o_ref.dtype)


def _dft_mid(mid, a4, spec4=None, mid_t=None):
    _, hk, n2, d = a4.shape
    tc = _tile(d, 4096, LANES)
    mspec = pl.BlockSpec((1, 2 * n2, 2 * n2), lambda k, j: (k, 0, 0))
    dspec = pl.BlockSpec((2, 1, n2, tc), lambda k, j: (0, k, 0, j))
    if spec4 is None:
        body, in_specs, operands, name = _dft_mid_body, [mspec, dspec], [mid.astype(BF16), a4], "hyena_dft_inner"
    else:
        body, in_specs = _dft_conv_body, [mspec, mspec, dspec, dspec]
        operands, name = [mid.astype(BF16), mid_t.astype(BF16), a4, spec4], "hyena_dft_inner_conv"
    return pl.pallas_call(
        body, grid=(hk, d // tc), in_specs=in_specs, out_specs=dspec,
        out_shape=jax.ShapeDtypeStruct(a4.shape, BF16), name=name,
        compiler_params=_params(("parallel", "arbitrary")),
    )(*operands)


def _row_tile(m):
    return _tile(m, 1024, 16)


def _short_conv(h, w_in, w_conv, w_out):
    m, d = h.shape
    tm = _row_tile(m)
    tn_in = _tile(d, 256, LANES)
    ncb = d // tn_in
    b, u = _matmul(h, [(w_in, 0), (w_in, ncb), (w_in, 2 * ncb)], [(d, tn_in, BF16), (d, tn_in, BF16)],
                   _ep_gate_pair, tm=tm, tn=tn_in, name="sc_in_proj")
    tn = _tile(d, 512, LANES)
    tmo = _tile(m, 512, HALO_ROWS)
    w8 = jnp.concatenate([w_conv.astype(F32), jnp.zeros((SUBLANES - 3, d), F32)], axis=0)
    main, prev, nxt = _row_halo_specs(tmo, d, m)
    lhs = [(b, *main), (u, *main), (u, *prev), (u, *nxt), (w8, (SUBLANES, d), lambda i, j, k: (0, 0))]
    return _matmul(lhs, [(w_out, 0)], [(d, tn, BF16)], _ep_plain, tm=tmo, tn=tn, name="sc_out_proj",
                   lhs_fn=_lhs_sc_gate, lhs_shape=(m, d))[0]


def _ffn(h2, w_gate_up, w_down):
    m, d = h2.shape
    f = w_down[0].shape[1]
    tm = _row_tile(m)
    tf = _tile(f, 512, LANES)
    a = _matmul(h2, [(w_gate_up, 0), (w_gate_up, f // tf)], [(f, tf, BF16)], _ep_swiglu,
                tm=tm, tn=tf, name="ffn_gate_up")[0]
    tk = f if f <= 4096 else _tile(f, 6144, LANES)
    tn = _tile(d, 512, LANES)
    return _matmul(a, [(w_down, 0)], [(d, tn, BF16)], _ep_plain, tm=tm, tn=tn, tk=tk, name="ffn_down")[0]


def _rope_tables(n, n_ctx, scale):
    rows = n // GRID_W
    row = jnp.repeat(jnp.arange(rows, dtype=F32), GRID_W)
    col = jnp.tile(jnp.arange(GRID_W, dtype=F32), rows)
    axis_dim = ROPE_DIM // 2
    inv = ROPE_BASE ** (-jnp.arange(0, axis_dim, 2, dtype=F32) / axis_dim)
    ang_r, ang_c = row[:, None] * inv, col[:, None] * inv
    cos = jnp.concatenate([jnp.cos(ang_r), jnp.cos(ang_c)], axis=1)
    sin = jnp.concatenate([jnp.sin(ang_r), jnp.sin(ang_c)], axis=1)
    rot = jnp.concatenate([cos, cos, -sin, sin], axis=1)
    tab_q = scale * jnp.concatenate([jnp.ones((n, NOPE_DIM), F32), rot], axis=1)
    ctx_rot = jnp.concatenate([jnp.ones((n_ctx, ROPE_DIM), F32), jnp.zeros((n_ctx, ROPE_DIM), F32)], axis=1)
    return tab_q, rot, ctx_rot


def _mla(h, hc, w_down, q_norm, kv_norm, w_uq, w_ukv, w_out):
    n, d = h.shape
    n_ctx = hc.shape[0]
    q_rank, kv_rank = w_uq.shape[0], w_ukv.shape[0]
    heads = w_uq.shape[1] // (NOPE_DIM + ROPE_DIM)
    scale = (NOPE_DIM + ROPE_DIM) ** -0.5 * math.log2(math.e)
    tab_q, tab_k, tab_kc = _rope_tables(n, n_ctx, scale)

    def rope_cols(w):
        qd = ROPE_DIM // 4
        a = jnp.concatenate([w[..., 0:qd], w[..., 2 * qd:3 * qd]], axis=-1)
        b = jnp.concatenate([w[..., qd:2 * qd], w[..., 3 * qd:4 * qd]], axis=-1)
        return jnp.concatenate([a, b, b, a], axis=-1)

    w_dq = w_down[:, :q_rank].astype(BF16)
    w_dkv = jnp.concatenate([w_down[:, q_rank:q_rank + kv_rank], rope_cols(w_down[:, q_rank + kv_rank:])],
                            axis=1).astype(BF16)
    wq3 = w_uq.reshape(q_rank, heads, NOPE_DIM + ROPE_DIM)
    w_uq_p = jnp.concatenate([wq3[:, :, :NOPE_DIM], rope_cols(wq3[:, :, NOPE_DIM:])], axis=2)
    qw = NOPE_DIM + 2 * ROPE_DIM
    w_uq_p = w_uq_p.reshape(q_rank, heads * qw).astype(BF16)
    wkv3 = w_ukv.reshape(kv_rank, heads, NOPE_DIM + V_DIM)
    w_kn = wkv3[:, :, :NOPE_DIM].reshape(kv_rank, heads * NOPE_DIM).astype(BF16)
    w_vt = wkv3[:, :, NOPE_DIM:].reshape(kv_rank, heads * V_DIM).T.astype(BF16)

    tm = _row_tile(n)
    cqn = _matmul(h, [(w_dq, 0)], [(q_rank, q_rank, BF16)], _ep_rms,
                  extras=[(q_norm.reshape(1, q_rank).astype(F32), (1, q_rank), lambda i, j, k: (0, 0))],
                  tm=tm, tn=q_rank, name="mla_q_down")[0]

    def kv_down(hh, tab, name):
        m = hh.shape[0]
        tmk = _row_tile(m)
        wd = kv_rank + 2 * ROPE_DIM
        return _matmul(hh, [(w_dkv, 0)], [(kv_rank, kv_rank, BF16), (LANES, LANES, BF16)],
                       functools.partial(_ep_kv_down, kv_rank=kv_rank),
                       extras=[(kv_norm.reshape(1, kv_rank).astype(F32), (1, kv_rank), lambda i, j, k: (0, 0)),
                               (tab, (tmk, 2 * ROPE_DIM), lambda i, j, k: (i, 0))],
                       tm=tmk, tn=wd, name=name)

    ckv, kr = kv_down(h, tab_k, "mla_kv_down")
    ckv_c, kr_c = kv_down(hc, tab_kc, "mla_kv_down_ctx")
    ckv = jnp.concatenate([ckv, ckv_c], axis=0)
    kr = jnp.concatenate([kr, kr_c], axis=0)
    nk = n + n_ctx

    gq = 2 if heads % 2 == 0 else 1
    q = _q_up_transposed(w_uq_p.T, cqn, tab_q.T)
    tmk = _tile(nk, 1664, 16)
    k = _matmul(ckv, [(w_kn, 0)], [(heads * qw, gq * qw, BF16)], _ep_k_up,
                extras=[(kr, (tmk, LANES), lambda i, j, k: (i, 0))], tm=tmk, tn=gq * NOPE_DIM, name="mla_k_up")[0]
    vt = _v_up_transposed(w_vt, ckv)
    o = _attention(q, k, vt, heads)
    tn = _tile(d, 512, LANES)
    return _matmul(o, [(w_out, 0)], [(d, tn, BF16)], _ep_plain, tm=tm, tn=tn, name="mla_out_proj")[0]


def _hyena(h, w_in, conv_w, conv_b, f_w1, f_b1, f_w2, f_b2, f_w3, f_b3, f_freq, f_w4, skip, w_out):
    n, d = h.shape
    tm = _row_tile(n)
    tn = _tile(d, 512, LANES)
    z = _matmul(h, [(w_in, 0)], [(3 * d, tn, BF16)], _ep_plain, tm=tm, tn=tn, name="hy_in_proj")[0]
    x0, vv = _hy_gate(z, conv_w, conv_b)
    n2 = DFT_INNER
    n1 = 2 * n // n2
    hk = n1 // 2
    fwd, inv, mid, mid_t = _dft_tables(n)
    g = _hyena_filter(n, d, f_w1, f_b1, f_w2, f_b2, f_w3, f_b3, f_freq, f_w4)
    tcol = _tile(n2 * d, 4096, LANES)
    ga = _dft_rows(fwd, g, tcol)
    spec = _dft_mid(mid, ga.reshape(2, hk, n2, d))
    va = _dft_rows(fwd[:, :hk], vv.reshape(hk, n2 * d), tcol)
    vb = _dft_mid(mid, va.reshape(2, hk, n2, d), spec, mid_t)
    y = _dft_rows(inv, vb.reshape(n1, n2 * d), tcol).reshape(n, d)
    tmo = _tile(n, 512, 16)
    row = ((tmo, d), lambda i, j, k: (i, 0))
    skip_rows = jnp.broadcast_to(skip.reshape(1, d).astype(BF16), (HALO_ROWS, d))
    lhs = [(x0, *row), (y, *row), (vv, *row), (skip_rows, (HALO_ROWS, d), lambda i, j, k: (0, 0))]
    return _matmul(lhs, [(w_out, 0)], [(d, tn, BF16)], _ep_plain, tm=tmo, tn=tn, name="hy_out_proj",
                   lhs_fn=_lhs_hy_gate, lhs_shape=(n, d))[0]


def kernel(x, c, ctx, c_ctx, ada_down, ada_up, ada_bias, norm_gain, ffn_w_gate_up, ffn_w_down, sc_w_in, sc_conv, sc_w_out, mla_w_down, mla_q_norm, mla_kv_norm, mla_w_uq, mla_w_ukv, mla_w_out, hy_w_in, hy_conv, hy_conv_b, hy_f_w1, hy_f_b1, hy_f_w2, hy_f_b2, hy_f_w3, hy_f_b3, hy_f_freq, hy_f_w4, hy_skip, hy_w_out):
    batch, n, d = x.shape
    assert batch == 1 and c.shape[0] == 1 and ctx.shape[0] == 1
    depth = ada_down.shape[0]
    n_mixers = 3
    xs = x.reshape(n, d)
    cs = ctx.reshape(ctx.shape[1], d)

    mla_layers = [i for i in range(depth) if i % n_mixers == 1]
    last_ctx_read = mla_layers[-1] if mla_layers else -1

    s_raw = jnp.concatenate([c.reshape(1, d), c_ctx.reshape(1, d), jnp.zeros((2 * SUBLANES - 2, d), F32)], axis=0)
    mods = _adaln(s_raw, ada_down, ada_up, ada_bias)

    def mod_vecs(i, row):
        return [mods[i, row, m * d:(m + 1) * d] for m in range(N_MOD)]

    ffn_gu, ffn_dn = ffn_w_gate_up, ffn_w_down.astype(BF16)
    sc_in, sc_out = sc_w_in.astype(BF16), sc_w_out.astype(BF16)
    hy_in, hy_out = hy_w_in.astype(BF16), hy_w_out.astype(BF16)
    mla_out = mla_w_out.astype(BF16)
    pend = None
    pend_c = None
    for i in range(depth):
        kind, j = i % n_mixers, i // n_mixers
        ctx_full = i < last_ctx_read
        ctx_keys = i == last_ctx_read
        g = norm_gain[i]
        streams = [(0, xs, pend)]
        if ctx_full or ctx_keys:
            streams.append((1, cs, pend_c))
        hs = {}
        cur = {}
        for row, xv, pd in streams:
            mv = mod_vecs(i, row)
            if pd is None:
                _, hh = _resid_norm_mod(xv, None, _vec_rows(d, g[0], g[0], g[0], mv[0], mv[1]), has_h=True)
            else:
                xv, hh = _resid_norm_mod(xv, pd[0], _vec_rows(d, pd[1], pd[2], g[0], mv[0], mv[1]), has_h=True)
            hs[row], cur[row] = hh, xv

        ys = {}
        if kind == 0:
            w_in, w_out = (sc_in, j), (sc_out, j)
            ys[0] = _short_conv(hs[0], w_in, sc_conv[j], w_out)
            if ctx_full:
                ys[1] = _short_conv(hs[1], w_in, sc_conv[j], w_out)
        elif kind == 1:
            ys[0] = _mla(hs[0], hs[1], mla_w_down[j], mla_q_norm[j], mla_kv_norm[j], mla_w_uq[j], mla_w_ukv[j],
                         (mla_out, j))
            assert not ctx_full
        else:
            hp = ((hy_in, j), hy_conv[j], hy_conv_b[j], hy_f_w1[j], hy_f_b1[j], hy_f_w2[j], hy_f_b2[j],
                  hy_f_w3[j], hy_f_b3[j], hy_f_freq[j], hy_f_w4[j], hy_skip[j], (hy_out, j))
            ys[0] = _hyena(hs[0], *hp)
            if ctx_full:
                ys[1] = _hyena(hs[1], *hp)

        w_gu, w_dn = (ffn_gu, i), (ffn_dn, i)
        new_pend = {0: None, 1: None}
        for row in ys:
            mv = mod_vecs(i, row)
            xv, h2 = _resid_norm_mod(cur[row], ys[row], _vec_rows(d, mv[2], g[1], g[2], mv[3], mv[4]), has_h=True)
            cur[row] = xv
            new_pend[row] = (_ffn(h2, w_gu, w_dn), mv[5], g[3])
        xs, pend = cur[0], new_pend[0]
        if ctx_full:
            cs, pend_c = cur[1], new_pend[1]
        else:
            pend_c = None

    xs, _ = _resid_norm_mod(xs, pend[0], _vec_rows(d, pend[1], pend[2], pend[2], pend[1], pend[1]), has_h=False)
    return xs.reshape(batch, n, d)
```

```python
import functools
import math

import jax
import jax.numpy as jnp
from jax import lax
from jax.experimental import pallas as pl
from jax.experimental.pallas import tpu as pltpu

F32 = jnp.float32
BF16 = jnp.bfloat16

EPS = 1e-6
N_MOD = 6
NOPE_DIM = 128
ROPE_DIM = 64
V_DIM = 128
GRID_W = 64
ROPE_BASE = 10000.0
HY_BANDS = 16
HY_DECAY_TARGET = 1e-2
HY_FAST_DECAY_PCT = 0.3
HY_SLOW_DECAY_PCT = 1.5

LANES = 128
SUBLANES = 8
VMEM_LIMIT_BYTES = 56 * 1024 * 1024
DFT_INNER = 128
HALO_ROWS = 16
MXU_DEPTH = 256

def _tile(dim, pref, align):
    best = None
    t = align
    while t <= min(dim, pref):
        if dim % t == 0:
            best = t
        t += align
    return best if best is not None else dim


def _params(sem):
    return pltpu.CompilerParams(dimension_semantics=sem, vmem_limit_bytes=VMEM_LIMIT_BYTES)


def _split_hi_lo(x):
    hi = x.astype(BF16)
    lo = (x - hi.astype(F32)).astype(BF16)
    return hi, lo


def _dot(a, b):
    return jnp.dot(a, b, preferred_element_type=F32)


def _dot3(a, b):
    ah, al = _split_hi_lo(a)
    bh, bl = _split_hi_lo(b)
    return _dot(ah, bh) + _dot(ah, bl) + _dot(al, bh)


def _rms(x, gain):
    return x * lax.rsqrt(jnp.mean(x * x, axis=-1, keepdims=True) + EPS) * gain


def _silu(x):
    return x * (1.0 / (1.0 + jnp.exp(-x)))


def _mm_body(*refs, n_lhs, n_w, n_extra, n_out, nk, epilogue, lhs_fn):
    refs = list(refs)
    lhs_refs = [refs.pop(0) for _ in range(n_lhs)]
    w_refs = [refs.pop(0) for _ in range(n_w)]
    extra_refs = [refs.pop(0) for _ in range(n_extra)]
    out_refs = [refs.pop(0) for _ in range(n_out)]
    acc_refs = [refs.pop(0) for _ in range(n_w if nk > 1 else 0)]
    if lhs_fn is None:
        a = lhs_refs[0][...].astype(BF16)
    else:
        lhs_scratch = refs.pop(0)

        @pl.when(pl.program_id(1) == 0)
        def _():
            lhs_fn(lhs_refs, lhs_scratch)

        a = lhs_scratch[...]
    dots = [_dot(a, w[...].astype(BF16)) for w in w_refs]
    if nk == 1:
        epilogue(dots, extra_refs, out_refs)
        return
    k = pl.program_id(2)

    @pl.when(k == 0)
    def _():
        for acc, d in zip(acc_refs, dots):
            acc[...] = d

    @pl.when(k > 0)
    def _():
        for acc, d in zip(acc_refs, dots):
            acc[...] += d

    @pl.when(k == nk - 1)
    def _():
        epilogue([acc[...] for acc in acc_refs], extra_refs, out_refs)


def _matmul(a, ws, outs, epilogue, *, extras=(), tm, tn, tk=None, name, lhs_fn=None, lhs_shape=None):
    if lhs_fn is None:
        M, K = a.shape
    else:
        M, K = lhs_shape
    tk = K if tk is None else tk
    nk = K // tk
    assert lhs_fn is None or nk == 1
    n_col_blocks = outs[0][0] // outs[0][1]
    grid = (M // tm, n_col_blocks, nk)
    if lhs_fn is None:
        in_specs = [pl.BlockSpec((tm, tk), lambda i, j, k: (i, k))]
        operands = [a]
    else:
        in_specs = [pl.BlockSpec(bshape, imap) for _, bshape, imap in a]
        operands = [arr for arr, _, _ in a]
    n_lhs = len(operands)
    for w, off in ws:
        if isinstance(w, tuple):
            w, layer = w
            in_specs.append(pl.BlockSpec((None, tk, tn),
                                         functools.partial(lambda i, j, k, o, l: (l, k, j + o), o=off, l=layer)))
        else:
            in_specs.append(pl.BlockSpec((tk, tn), functools.partial(lambda i, j, k, o: (k, j + o), o=off)))
        operands.append(w)
    for arr, bshape, imap in extras:
        in_specs.append(pl.BlockSpec(bshape, imap))
        operands.append(arr)
    out_shape = [jax.ShapeDtypeStruct((M, wt), dt) for wt, _, dt in outs]
    out_specs = [pl.BlockSpec((tm, bw), lambda i, j, k: (i, j)) for _, bw, _ in outs]
    scratch = [pltpu.VMEM((tm, tn), F32) for _ in ws] if nk > 1 else []
    if lhs_fn is not None:
        scratch.append(pltpu.VMEM((tm, K), BF16))
    body = functools.partial(_mm_body, n_lhs=n_lhs, n_w=len(ws), n_extra=len(extras), n_out=len(outs), nk=nk,
                             epilogue=epilogue, lhs_fn=lhs_fn)
    res = pl.pallas_call(
        body, grid=grid, in_specs=in_specs, out_specs=out_specs, out_shape=out_shape,
        scratch_shapes=scratch, name=name,
        compiler_params=_params(("parallel", "arbitrary", "arbitrary")),
    )(*operands)
    return res


def _ep_plain(dots, extras, outs):
    outs[0][...] = dots[0].astype(outs[0].dtype)


def _ep_gate_pair(dots, extras, outs):
    outs[0][...] = dots[0].astype(outs[0].dtype)
    outs[1][...] = (dots[1] * dots[2]).astype(outs[1].dtype)


def _ep_swiglu(dots, extras, outs):
    outs[0][...] = (_silu(dots[0]) * dots[1]).astype(outs[0].dtype)


def _ep_rms(dots, extras, outs):
    outs[0][...] = _rms(dots[0], extras[0][...]).astype(outs[0].dtype)


def _ep_kv_down(dots, extras, outs, *, kv_rank):
    gain_ref, tab_ref = extras
    d = dots[0]
    outs[0][...] = _rms(d[:, :kv_rank], gain_ref[...]).astype(outs[0].dtype)
    p = d[:, kv_rank:] * tab_ref[...]
    outs[1][...] = (p + pltpu.roll(p, ROPE_DIM, axis=1)).astype(outs[1].dtype)


def _ep_k_up(dots, extras, outs):
    kn = dots[0]
    kr = extras[0][...]
    for g in range(kn.shape[1] // NOPE_DIM):
        base = g * (NOPE_DIM + LANES)
        outs[0][:, base:base + NOPE_DIM] = kn[:, g * NOPE_DIM:(g + 1) * NOPE_DIM].astype(outs[0].dtype)
        outs[0][:, base + NOPE_DIM:base + NOPE_DIM + LANES] = kr


def _adaln_body(s_ref, down_ref, up_ref, bias_ref, out_ref, t_ref):
    @pl.when(pl.program_id(1) == 0)
    def _():
        t_ref[...] = _dot3(_silu(s_ref[...]), down_ref[0])

    out_ref[0] = _dot3(t_ref[...], up_ref[0]) + bias_ref[0]


def _adaln(s_raw, ada_down, ada_up, ada_bias):
    depth, d, r = ada_down.shape
    n6 = ada_up.shape[2]
    rows = s_raw.shape[0]
    tn = _tile(n6, 2048, LANES)
    return pl.pallas_call(
        _adaln_body, grid=(depth, n6 // tn),
        in_specs=[pl.BlockSpec((rows, d), lambda i, j: (0, 0)),
                  pl.BlockSpec((1, d, r), lambda i, j: (i, 0, 0)),
                  pl.BlockSpec((1, r, tn), lambda i, j: (i, 0, j)),
                  pl.BlockSpec((1, 1, tn), lambda i, j: (i, 0, j))],
        out_specs=pl.BlockSpec((1, rows, tn), lambda i, j: (i, 0, j)),
        out_shape=jax.ShapeDtypeStruct((depth, rows, n6), F32),
        scratch_shapes=[pltpu.VMEM((rows, r), F32)], name="adaln",
        compiler_params=_params(("arbitrary", "arbitrary")),
    )(s_raw, ada_down, ada_up, ada_bias.reshape(depth, 1, n6))


def _rnm_body(*refs, has_resid, has_h):
    refs = list(refs)
    x_ref = refs.pop(0)
    y_ref = refs.pop(0) if has_resid else None
    vec_ref = refs.pop(0)
    x = x_ref[...]
    if has_resid:
        xo_ref = refs.pop(0)
        x = x + vec_ref[0:1, :] * _rms(y_ref[...].astype(F32), vec_ref[1:2, :])
        xo_ref[...] = x
    if has_h:
        h_ref = refs.pop(0)
        h_ref[...] = (_rms(x, vec_ref[2:3, :]) * (1.0 + vec_ref[4:5, :]) + vec_ref[3:4, :]).astype(h_ref.dtype)


def _resid_norm_mod(x, y, vec, *, has_h, rows=None):
    n, d = x.shape
    n = n if rows is None else rows
    has_resid = y is not None
    tr = _tile(n, 256, 16)
    spec = pl.BlockSpec((tr, d), lambda i: (i, 0))
    in_specs = [spec] + ([spec] if has_resid else []) + [pl.BlockSpec(vec.shape, lambda i: (0, 0))]
    operands = [x] + ([y] if has_resid else []) + [vec]
    out_shape, out_specs = [], []
    if has_resid:
        out_shape.append(jax.ShapeDtypeStruct((n, d), F32))
        out_specs.append(spec)
    if has_h:
        out_shape.append(jax.ShapeDtypeStruct((n, d), BF16))
        out_specs.append(spec)
    res = pl.pallas_call(
        functools.partial(_rnm_body, has_resid=has_resid, has_h=has_h), grid=(n // tr,),
        in_specs=in_specs, out_specs=out_specs, out_shape=out_shape, name="resid_norm_mod",
        compiler_params=_params(("parallel",)),
    )(*operands)
    res = list(res)
    x_new = res.pop(0) if has_resid else None
    h = res.pop(0) if has_h else None
    return x_new, h


def _vec_rows(d, *rows):
    out = [r.reshape(1, d).astype(F32) for r in rows]
    out += [jnp.zeros((1, d), F32)] * (SUBLANES - len(out))
    return jnp.concatenate(out, axis=0)


def _conv3_rows(main, prev_blk, next_blk, w0, w1, w2, is_first, is_last):
    main = main.astype(F32)
    tr = main.shape[0]
    prev_row = jnp.where(is_first, 0.0, prev_blk[HALO_ROWS - 1:HALO_ROWS, :].astype(F32))
    next_row = jnp.where(is_last, 0.0, next_blk[0:1, :].astype(F32))
    up = pltpu.roll(main, 1, axis=0)
    dn = pltpu.roll(main, tr - 1, axis=0)
    rows = lax.broadcasted_iota(jnp.int32, (SUBLANES, main.shape[1]), 0)
    up = jnp.concatenate([jnp.where(rows == 0, prev_row, up[:SUBLANES]), up[SUBLANES:]], axis=0)
    dn = jnp.concatenate([dn[:tr - SUBLANES], jnp.where(rows == SUBLANES - 1, next_row, dn[tr - SUBLANES:])], axis=0)
    return w0 * up + w1 * main + w2 * dn


def _halo_specs(tr, tc, n_rows, col_off):
    per = tr // HALO_ROWS
    last = n_rows // HALO_ROWS - 1
    return [
        pl.BlockSpec((tr, tc), lambda i, j: (i, j + col_off)),
        pl.BlockSpec((HALO_ROWS, tc), lambda i, j: (jnp.maximum(i * per - 1, 0), j + col_off)),
        pl.BlockSpec((HALO_ROWS, tc), lambda i, j: (jnp.minimum((i + 1) * per, last), j + col_off)),
    ]


LHS_CHUNK = 512


def _lhs_sc_gate(refs, scratch):
    b_ref, u_ref, up_ref, un_ref, w_ref = refs
    i = pl.program_id(0)
    first, last = i == 0, i == pl.num_programs(0) - 1
    width = scratch.shape[1]
    cw = min(width, LHS_CHUNK)
    for c in range(0, width, cw):
        sl = slice(c, c + cw)
        conv = _conv3_rows(u_ref[:, sl], up_ref[:, sl], un_ref[:, sl], w_ref[0:1, sl], w_ref[1:2, sl],
                           w_ref[2:3, sl], first, last)
        scratch[:, sl] = (b_ref[:, sl].astype(F32) * conv).astype(scratch.dtype)


def _lhs_hy_gate(refs, scratch):
    x0_ref, y_ref, v_ref, skip_ref = refs
    width = scratch.shape[1]
    cw = min(width, LHS_CHUNK)
    for c in range(0, width, cw):
        sl = slice(c, c + cw)
        yv = y_ref[:, sl] + v_ref[:, sl] * skip_ref[0:1, sl]
        scratch[:, sl] = x0_ref[:, sl] * yv


def _row_halo_specs(tm, d, n_rows):
    per = tm // HALO_ROWS
    last = n_rows // HALO_ROWS - 1
    return [
        ((tm, d), lambda i, j, k: (i, 0)),
        ((HALO_ROWS, d), lambda i, j, k: (jnp.maximum(i * per - 1, 0), 0)),
        ((HALO_ROWS, d), lambda i, j, k: (jnp.minimum((i + 1) * per, last), 0)),
    ]


def _hy_gate_body(*refs):
    z_refs = refs[0:9]
    w_refs = refs[9:12]
    x0_ref, vv_ref = refs[12:14]
    i = pl.program_id(0)
    first, last = i == 0, i == pl.num_programs(0) - 1
    conv = []
    for g in range(3):
        m, p, nx = z_refs[3 * g:3 * g + 3]
        w = w_refs[g]
        conv.append(_conv3_rows(m[...], p[...], nx[...], w[0:1, :], w[1:2, :], w[2:3, :], first, last) + w[3:4, :])
    x0_ref[...] = conv[0].astype(x0_ref.dtype)
    vv_ref[...] = (conv[1] * conv[2]).astype(vv_ref.dtype)


def _hy_gate(z, conv_w, conv_b):
    n, d3 = z.shape
    d = d3 // 3
    tr = _tile(n, 512, 16)
    tc = _tile(d, 512, LANES)
    ncb = d // tc
    w8 = jnp.concatenate([conv_w.astype(F32), conv_b.reshape(1, d3).astype(F32),
                          jnp.zeros((SUBLANES - 4, d3), F32)], axis=0)
    in_specs, operands = [], []
    for g in range(3):
        in_specs += _halo_specs(tr, tc, n, g * ncb)
        operands += [z, z, z]
    for g in range(3):
        in_specs.append(pl.BlockSpec((SUBLANES, tc), functools.partial(lambda i, j, o: (0, j + o), o=g * ncb)))
        operands.append(w8)
    spec = pl.BlockSpec((tr, tc), lambda i, j: (i, j))
    return pl.pallas_call(
        _hy_gate_body, grid=(n // tr, ncb), in_specs=in_specs, out_specs=[spec, spec],
        out_shape=[jax.ShapeDtypeStruct((n, d), BF16)] * 2, name="hy_conv_gate",
        compiler_params=_params(("parallel", "parallel")),
    )(*operands)


def _attn_body(q_ref, k_ref, vt_ref, o_ref, m_ref, l_ref, acc_ref, *, ck):
    ki = pl.program_id(2)

    @pl.when(ki == 0)
    def _():
        m_ref[...] = jnp.full(m_ref.shape, -jnp.inf, F32)
        l_ref[...] = jnp.zeros(l_ref.shape, F32)
        acc_ref[...] = jnp.zeros(acc_ref.shape, F32)

    tq = q_ref.shape[1]
    tk = k_ref.shape[0]
    n_streams = 4 if tq % (4 * MXU_DEPTH) == 0 else 1
    sw = tq // n_streams
    lanes = [slice(h * sw, (h + 1) * sw) for h in range(n_streams)]
    units = tk // MXU_DEPTH if tk % MXU_DEPTH == 0 else 0
    if units:
        nch = max(1, units * MXU_DEPTH // ck)
        sizes = [(units // nch + (1 if c >= nch - units % nch else 0)) * MXU_DEPTH for c in range(nch)]
    else:
        sizes = [min(ck, tk - lo) for lo in range(0, tk, ck)]
    starts = [sum(sizes[:c]) for c in range(len(sizes))]
    bounds = [(lo, lo + sz) for lo, sz in zip(starts, sizes)]

    def scores(b, h):
        return _dot(k_ref[b[0]:b[1], :], q_ref[:, lanes[h]])

    def value_update(acc, pend):
        a_prev, p_prev, b_prev = pend
        return a_prev * acc + _dot(vt_ref[:, b_prev[0]:b_prev[1]], p_prev)

    m = [m_ref[:, ln] for ln in lanes]
    l = [l_ref[:, ln] for ln in lanes]
    acc = [acc_ref[:, ln] for ln in lanes]
    s_next = [scores(bounds[0], h) for h in range(n_streams)]
    pending = [None] * n_streams
    for c, b in enumerate(bounds):
        for h in range(n_streams):
            s = s_next[h]
            if c + 1 < len(bounds):
                s_next[h] = scores(bounds[c + 1], h)
            if pending[h] is not None:
                acc[h] = value_update(acc[h], pending[h])
            m_new = jnp.maximum(m[h], jnp.max(s, axis=0, keepdims=True))
            alpha = jnp.exp2(m[h] - m_new)
            p = jnp.exp2(s - m_new)
            l[h] = alpha * l[h] + jnp.sum(p, axis=0, keepdims=True)
            pending[h] = (alpha, p.astype(BF16), b)
            m[h] = m_new
    for h in range(n_streams):
        acc[h] = value_update(acc[h], pending[h])
        m_ref[:, lanes[h]], l_ref[:, lanes[h]], acc_ref[:, lanes[h]] = m[h], l[h], acc[h]

    @pl.when(ki == pl.num_programs(2) - 1)
    def _():
        for h in range(n_streams):
            o_ref[lanes[h], :] = (acc[h] * (1.0 / l[h])).T.astype(o_ref.dtype)


def _attention(q, k, vt, heads):
    n = q.shape[1]
    nk = k.shape[0]
    qw = q.shape[0] // heads
    unit = MXU_DEPTH if nk % MXU_DEPTH == 0 else LANES
    ck = 6 * MXU_DEPTH
    tq = _tile(n, 2048, LANES)
    tk = _tile(nk, 13 * MXU_DEPTH, unit)
    return pl.pallas_call(
        functools.partial(_attn_body, ck=ck), grid=(heads, n // tq, nk // tk),
        in_specs=[pl.BlockSpec((qw, tq), lambda h, i, j: (h, i)),
                  pl.BlockSpec((tk, qw), lambda h, i, j: (j, h)),
                  pl.BlockSpec((V_DIM, tk), lambda h, i, j: (h, j))],
        out_specs=pl.BlockSpec((tq, V_DIM), lambda h, i, j: (i, h)),
        out_shape=jax.ShapeDtypeStruct((n, heads * V_DIM), BF16),
        scratch_shapes=[pltpu.VMEM((1, tq), F32), pltpu.VMEM((1, tq), F32), pltpu.VMEM((V_DIM, tq), F32)],
        name="mla_flash_attention",
        compiler_params=_params(("parallel", "parallel", "arbitrary")),
    )(q, k, vt)


def _qt_body(w_ref, c_ref, tab_ref, o_ref):
    x = lax.dot_general(w_ref[...], c_ref[...], (((1,), (1,)), ((), ())), preferred_element_type=F32)
    tab = tab_ref[...]
    width = tab.shape[0]
    for g in range(x.shape[0] // width):
        o_ref[g * width:(g + 1) * width, :] = (x[g * width:(g + 1) * width] * tab).astype(o_ref.dtype)


def _q_up_transposed(w_qt, cqn, tab_t):
    hq, r = w_qt.shape
    n = cqn.shape[0]
    qw = tab_t.shape[0]
    th = _tile(hq, 4 * qw, qw)
    tn = _tile(n, 1024, LANES)
    return pl.pallas_call(
        _qt_body, grid=(hq // th, n // tn),
        in_specs=[pl.BlockSpec((th, r), lambda i, j: (i, 0)), pl.BlockSpec((tn, r), lambda i, j: (j, 0)),
                  pl.BlockSpec((qw, tn), lambda i, j: (0, j))],
        out_specs=pl.BlockSpec((th, tn), lambda i, j: (i, j)),
        out_shape=jax.ShapeDtypeStruct((hq, n), BF16), name="mla_q_up_transposed",
        compiler_params=_params(("parallel", "parallel")),
    )(w_qt, cqn, tab_t)


def _vt_body(w_ref, c_ref, o_ref):
    o_ref[...] = lax.dot_general(w_ref[...], c_ref[...], (((1,), (1,)), ((), ())),
                                 preferred_element_type=F32).astype(o_ref.dtype)


def _v_up_transposed(w_vt, ckv):
    hv, r = w_vt.shape
    nk = ckv.shape[0]
    th = _tile(hv, 2 * V_DIM, V_DIM)
    tn = _tile(nk, 13 * MXU_DEPTH, LANES)
    return pl.pallas_call(
        _vt_body, grid=(hv // th, nk // tn),
        in_specs=[pl.BlockSpec((th, r), lambda i, j: (i, 0)), pl.BlockSpec((tn, r), lambda i, j: (j, 0))],
        out_specs=pl.BlockSpec((th, tn), lambda i, j: (i, j)),
        out_shape=jax.ShapeDtypeStruct((hv, nk), BF16), name="mla_v_up_transposed",
        compiler_params=_params(("parallel", "parallel")),
    )(w_vt, ckv)


def _filter_body(z_ref, w1_ref, w2_ref, w3_ref, bf_ref, w4f_ref, w4b_ref, dl_ref, o_ref, h_ref):
    @pl.when(pl.program_id(1) == 0)
    def _():
        bf = bf_ref[...]
        h = jnp.sin(bf[3:4, :] * (_dot3(z_ref[0], w1_ref[...]) + bf[0:1, :]))
        h = jnp.sin(bf[4:5, :] * (_dot3(h, w2_ref[...]) + bf[1:2, :]))
        h_ref[...] = jnp.sin(bf[5:6, :] * (_dot3(h, w3_ref[...]) + bf[2:3, :]))

    z = z_ref[0]
    t = z[:, 0:1]
    sign = z[:, HY_BANDS * 2 + 1:HY_BANDS * 2 + 2]
    h = h_ref[...].astype(BF16)
    half = h.shape[0] // 2
    filt = jnp.concatenate([_dot(h[:half], w4f_ref[...].astype(BF16)), _dot(h[half:], w4b_ref[...].astype(BF16))],
                           axis=0)
    o_ref[...] = (sign * filt * jnp.exp(-t * dl_ref[...])).astype(o_ref.dtype)


def _hyena_filter(n, d, f_w1, f_b1, f_w2, f_b2, f_w3, f_b3, f_freq, f_w4):
    fh = f_w1.shape[1]
    emb = f_w1.shape[0]
    r = jnp.arange(2 * n, dtype=jnp.int32)
    p = jnp.minimum(jnp.where(r < n, r, 2 * n - r), n - 1)
    pf = p.astype(F32)[:, None]
    t = pf / (n - 1)
    w = (2.0 * math.pi / n) * pf
    bands = jnp.linspace(1e-4, HY_BANDS - 1, HY_BANDS, dtype=F32)
    sign = jnp.where(r < n, 1.0, jnp.where(r == n, 0.0, -1.0)).astype(F32)[:, None]
    zw = LANES // 2
    z = jnp.concatenate([t, jnp.cos(bands * w), -jnp.sin(bands * w), sign,
                         jnp.zeros((2 * n, zw - emb - 1), F32)], axis=-1)
    w1p = jnp.concatenate([f_w1.astype(F32), jnp.zeros((zw - emb, fh), F32)], axis=0)
    bf = jnp.concatenate([f_b1.reshape(1, fh), f_b2.reshape(1, fh), f_b3.reshape(1, fh),
                          f_freq.reshape(3, fh), jnp.zeros((2, fh), F32)], axis=0).astype(F32)
    deltas = jnp.abs(jnp.linspace(math.log(HY_DECAY_TARGET) / HY_SLOW_DECAY_PCT,
                                  math.log(HY_DECAY_TARGET) / HY_FAST_DECAY_PCT, d, dtype=F32)).reshape(1, d)
    n2 = DFT_INNER
    n1 = 2 * n // n2
    zt = z.reshape(n1, n2, zw).transpose(1, 0, 2)
    tc = _tile(d, 2048, LANES)
    ncb = d // tc
    const = lambda shape: pl.BlockSpec(shape, lambda i, j: (0, 0))
    w4 = f_w4.astype(F32)
    return pl.pallas_call(
        _filter_body, grid=(n2, ncb),
        in_specs=[pl.BlockSpec((1, n1, zw), lambda i, j: (i, 0, 0)), const((zw, fh)), const((fh, fh)),
                  const((fh, fh)), const((SUBLANES, fh)),
                  pl.BlockSpec((fh, tc), lambda i, j: (0, j)),
                  pl.BlockSpec((fh, tc), lambda i, j: (0, j + ncb)),
                  pl.BlockSpec((1, tc), lambda i, j: (0, j))],
        out_specs=pl.BlockSpec((n1, tc), lambda i, j: (0, i * ncb + j)),
        out_shape=jax.ShapeDtypeStruct((n1, n2 * d), BF16),
        scratch_shapes=[pltpu.VMEM((n1, fh), F32)], name="hyena_filter",
        compiler_params=_params(("parallel", "arbitrary")),
    )(zt, w1p, f_w2.astype(F32), f_w3.astype(F32), bf, w4, w4, deltas)


def _dft_tables(n):
    big_n = 2 * n
    n2 = DFT_INNER
    n1 = big_n // n2
    hk = n1 // 2
    k1 = jnp.arange(hk, dtype=jnp.int32)
    m1 = jnp.arange(n1, dtype=jnp.int32)
    ang_a = (math.pi / n1) * ((m1[None, :] * (2 * k1[:, None] + 1)) % (2 * n1)).astype(F32)
    fwd = jnp.concatenate([jnp.cos(ang_a), -jnp.sin(ang_a)], axis=0)
    inv = (2.0 / big_n) * jnp.concatenate([jnp.cos(ang_a[:, :hk]).T, -jnp.sin(ang_a[:, :hk]).T], axis=1)
    k2 = jnp.arange(n2, dtype=jnp.int32)
    m2 = jnp.arange(n2, dtype=jnp.int32)
    freq = 2 * k1[:, None, None] + 1 + 2 * n1 * k2[None, :, None]
    ang_c = (math.pi / big_n) * ((m2[None, None, :] * freq) % (2 * big_n)).astype(F32)
    gr, gi = jnp.cos(ang_c), -jnp.sin(ang_c)
    mid = jnp.concatenate([jnp.concatenate([gr, -gi], axis=2), jnp.concatenate([gi, gr], axis=2)], axis=1)
    mid_t = jnp.swapaxes(mid, 1, 2)
    return fwd, inv, mid, mid_t


def _dft_rows_body(f_ref, x_ref, o_ref):
    o_ref[...] = _dot(f_ref[...], x_ref[...]).astype(o_ref.dtype)


def _dft_rows(f, x2d, tn):
    r, k = f.shape
    c = x2d.shape[1]
    return pl.pallas_call(
        _dft_rows_body, grid=(c // tn,),
        in_specs=[pl.BlockSpec((r, k), lambda j: (0, 0)), pl.BlockSpec((k, tn), lambda j: (0, j))],
        out_specs=pl.BlockSpec((r, tn), lambda j: (0, j)),
        out_shape=jax.ShapeDtypeStruct((r, c), BF16), name="hyena_dft_outer",
        compiler_params=_params(("parallel",)),
    )(f.astype(BF16), x2d)


def _stack_re_im(ref):
    _, _, n2, tc = ref.shape
    return ref[:, 0].reshape(2 * n2, tc)


def _dft_mid_body(m_ref, a_ref, o_ref):
    n2 = a_ref.shape[2]
    s = _dot(m_ref[0], _stack_re_im(a_ref))
    o_ref[0, 0] = s[:n2].astype(o_ref.dtype)
    o_ref[1, 0] = s[n2:].astype(o_ref.dtype)


def _dft_conv_body(m_ref, t_ref, a_ref, g_ref, o_ref):
    n2 = a_ref.shape[2]
    s = _dot(m_ref[0], _stack_re_im(a_ref))
    sr, si = s[:n2], s[n2:]
    gr, gi = g_ref[0, 0].astype(F32), g_ref[1, 0].astype(F32)
    y = jnp.concatenate([sr * gr - si * gi, sr * gi + si * gr], axis=0).astype(BF16)
    b = _dot(t_ref[0], y)
    o_ref[0, 0] = b[:n2].astype(o_ref.dtype)
    o_ref[1, 0] = b[n2:].astype(o_ref.dtype)


def _dft_mid(mid, a4, spec4=None, mid_t=None):
    _, hk, n2, d = a4.shape
    tc = _tile(d, 4096, LANES)
    mspec = pl.BlockSpec((1, 2 * n2, 2 * n2), lambda k, j: (k, 0, 0))
    dspec = pl.BlockSpec((2, 1, n2, tc), lambda k, j: (0, k, 0, j))
    if spec4 is None:
        body, in_specs, operands, name = _dft_mid_body, [mspec, dspec], [mid.astype(BF16), a4], "hyena_dft_inner"
    else:
        body, in_specs = _dft_conv_body, [mspec, mspec, dspec, dspec]
        operands, name = [mid.astype(BF16), mid_t.astype(BF16), a4, spec4], "hyena_dft_inner_conv"
    return pl.pallas_call(
        body, grid=(hk, d // tc), in_specs=in_specs, out_specs=dspec,
        out_shape=jax.ShapeDtypeStruct(a4.shape, BF16), name=name,
        compiler_params=_params(("parallel", "arbitrary")),
    )(*operands)


def _row_tile(m):
    return _tile(m, 1024, 16)


def _short_conv(h, w_in, w_conv, w_out):
    m, d = h.shape
    tm = _row_tile(m)
    tn_in = _tile(d, 256, LANES)
    ncb = d // tn_in
    b, u = _matmul(h, [(w_in, 0), (w_in, ncb), (w_in, 2 * ncb)], [(d, tn_in, BF16), (d, tn_in, BF16)],
                   _ep_gate_pair, tm=tm, tn=tn_in, name="sc_in_proj")
    tn = _tile(d, 512, LANES)
    tmo = _tile(m, 512, HALO_ROWS)
    w8 = jnp.concatenate([w_conv.astype(F32), jnp.zeros((SUBLANES - 3, d), F32)], axis=0)
    main, prev, nxt = _row_halo_specs(tmo, d, m)
    lhs = [(b, *main), (u, *main), (u, *prev), (u, *nxt), (w8, (SUBLANES, d), lambda i, j, k: (0, 0))]
    return _matmul(lhs, [(w_out, 0)], [(d, tn, BF16)], _ep_plain, tm=tmo, tn=tn, name="sc_out_proj",
                   lhs_fn=_lhs_sc_gate, lhs_shape=(m, d))[0]


def _ffn(h2, w_gate_up, w_down):
    m, d = h2.shape
    f = w_down[0].shape[1]
    tm = _row_tile(m)
    tf = _tile(f, 512, LANES)
    a = _matmul(h2, [(w_gate_up, 0), (w_gate_up, f // tf)], [(f, tf, BF16)], _ep_swiglu,
                tm=tm, tn=tf, name="ffn_gate_up")[0]
    tk = f if f <= 4096 else _tile(f, 6144, LANES)
    tn = _tile(d, 512, LANES)
    return _matmul(a, [(w_down, 0)], [(d, tn, BF16)], _ep_plain, tm=tm, tn=tn, tk=tk, name="ffn_down")[0]


def _rope_tables(n, n_ctx, scale):
    rows = n // GRID_W
    row = jnp.repeat(jnp.arange(rows, dtype=F32), GRID_W)
    col = jnp.tile(jnp.arange(GRID_W, dtype=F32), rows)
    axis_dim = ROPE_DIM // 2
    inv = ROPE_BASE ** (-jnp.arange(0, axis_dim, 2, dtype=F32) / axis_dim)
    ang_r, ang_c = row[:, None] * inv, col[:, None] * inv
    cos = jnp.concatenate([jnp.cos(ang_r), jnp.cos(ang_c)], axis=1)
    sin = jnp.concatenate([jnp.sin(ang_r), jnp.sin(ang_c)], axis=1)
    rot = jnp.concatenate([cos, cos, -sin, sin], axis=1)
    tab_q = scale * jnp.concatenate([jnp.ones((n, NOPE_DIM), F32), rot], axis=1)
    ctx_rot = jnp.concatenate([jnp.ones((n_ctx, ROPE_DIM), F32), jnp.zeros((n_ctx, ROPE_DIM), F32)], axis=1)
    return tab_q, rot, ctx_rot


def _mla(h, hc, w_down, q_norm, kv_norm, w_uq, w_ukv, w_out):
    n, d = h.shape
    n_ctx = hc.shape[0]
    q_rank, kv_rank = w_uq.shape[0], w_ukv.shape[0]
    heads = w_uq.shape[1] // (NOPE_DIM + ROPE_DIM)
    scale = (NOPE_DIM + ROPE_DIM) ** -0.5 * math.log2(math.e)
    tab_q, tab_k, tab_kc = _rope_tables(n, n_ctx, scale)

    def rope_cols(w):
        qd = ROPE_DIM // 4
        a = jnp.concatenate([w[..., 0:qd], w[..., 2 * qd:3 * qd]], axis=-1)
        b = jnp.concatenate([w[..., qd:2 * qd], w[..., 3 * qd:4 * qd]], axis=-1)
        return jnp.concatenate([a, b, b, a], axis=-1)

    w_dq = w_down[:, :q_rank].astype(BF16)
    w_dkv = jnp.concatenate([w_down[:, q_rank:q_rank + kv_rank], rope_cols(w_down[:, q_rank + kv_rank:])],
                            axis=1).astype(BF16)
    wq3 = w_uq.reshape(q_rank, heads, NOPE_DIM + ROPE_DIM)
    w_uq_p = jnp.concatenate([wq3[:, :, :NOPE_DIM], rope_cols(wq3[:, :, NOPE_DIM:])], axis=2)
    qw = NOPE_DIM + 2 * ROPE_DIM
    w_uq_p = w_uq_p.reshape(q_rank, heads * qw).astype(BF16)
    wkv3 = w_ukv.reshape(kv_rank, heads, NOPE_DIM + V_DIM)
    w_kn = wkv3[:, :, :NOPE_DIM].reshape(kv_rank, heads * NOPE_DIM).astype(BF16)
    w_vt = wkv3[:, :, NOPE_DIM:].reshape(kv_rank, heads * V_DIM).T.astype(BF16)

    tm = _row_tile(n)
    cqn = _matmul(h, [(w_dq, 0)], [(q_rank, q_rank, BF16)], _ep_rms,
                  extras=[(q_norm.reshape(1, q_rank).astype(F32), (1, q_rank), lambda i, j, k: (0, 0))],
                  tm=tm, tn=q_rank, name="mla_q_down")[0]

    def kv_down(hh, tab, name):
        m = hh.shape[0]
        tmk = _row_tile(m)
        wd = kv_rank + 2 * ROPE_DIM
        return _matmul(hh, [(w_dkv, 0)], [(kv_rank, kv_rank, BF16), (LANES, LANES, BF16)],
                       functools.partial(_ep_kv_down, kv_rank=kv_rank),
                       extras=[(kv_norm.reshape(1, kv_rank).astype(F32), (1, kv_rank), lambda i, j, k: (0, 0)),
                               (tab, (tmk, 2 * ROPE_DIM), lambda i, j, k: (i, 0))],
                       tm=tmk, tn=wd, name=name)

    ckv, kr = kv_down(h, tab_k, "mla_kv_down")
    ckv_c, kr_c = kv_down(hc, tab_kc, "mla_kv_down_ctx")
    ckv = jnp.concatenate([ckv, ckv_c], axis=0)
    kr = jnp.concatenate([kr, kr_c], axis=0)
    nk = n + n_ctx

    gq = 2 if heads % 2 == 0 else 1
    q = _q_up_transposed(w_uq_p.T, cqn, tab_q.T)
    tmk = _tile(nk, 1664, 16)
    k = _matmul(ckv, [(w_kn, 0)], [(heads * qw, gq * qw, BF16)], _ep_k_up,
                extras=[(kr, (tmk, LANES), lambda i, j, k: (i, 0))], tm=tmk, tn=gq * NOPE_DIM, name="mla_k_up")[0]
    vt = _v_up_transposed(w_vt, ckv)
    o = _attention(q, k, vt, heads)
    tn = _tile(d, 512, LANES)
    return _matmul(o, [(w_out, 0)], [(d, tn, BF16)], _ep_plain, tm=tm, tn=tn, name="mla_out_proj")[0]


def _hyena(h, w_in, conv_w, conv_b, f_w1, f_b1, f_w2, f_b2, f_w3, f_b3, f_freq, f_w4, skip, w_out):
    n, d = h.shape
    tm = _row_tile(n)
    tn = _tile(d, 512, LANES)
    z = _matmul(h, [(w_in, 0)], [(3 * d, tn, BF16)], _ep_plain, tm=tm, tn=tn, name="hy_in_proj")[0]
    x0, vv = _hy_gate(z, conv_w, conv_b)
    n2 = DFT_INNER
    n1 = 2 * n // n2
    hk = n1 // 2
    fwd, inv, mid, mid_t = _dft_tables(n)
    g = _hyena_filter(n, d, f_w1, f_b1, f_w2, f_b2, f_w3, f_b3, f_freq, f_w4)
    tcol = _tile(n2 * d, 4096, LANES)
    ga = _dft_rows(fwd, g, tcol)
    spec = _dft_mid(mid, ga.reshape(2, hk, n2, d))
    va = _dft_rows(fwd[:, :hk], vv.reshape(hk, n2 * d), tcol)
    vb = _dft_mid(mid, va.reshape(2, hk, n2, d), spec, mid_t)
    y = _dft_rows(inv, vb.reshape(n1, n2 * d), tcol).reshape(n, d)
    tmo = _tile(n, 512, 16)
    row = ((tmo, d), lambda i, j, k: (i, 0))
    skip_rows = jnp.broadcast_to(skip.reshape(1, d).astype(BF16), (HALO_ROWS, d))
    lhs = [(x0, *row), (y, *row), (vv, *row), (skip_rows, (HALO_ROWS, d), lambda i, j, k: (0, 0))]
    return _matmul(lhs, [(w_out, 0)], [(d, tn, BF16)], _ep_plain, tm=tmo, tn=tn, name="hy_out_proj",
                   lhs_fn=_lhs_hy_gate, lhs_shape=(n, d))[0]


def kernel(x, c, ctx, c_ctx, ada_down, ada_up, ada_bias, norm_gain, ffn_w_gate_up, ffn_w_down, sc_w_in, sc_conv, sc_w_out, mla_w_down, mla_q_norm, mla_kv_norm, mla_w_uq, mla_w_ukv, mla_w_out, hy_w_in, hy_conv, hy_conv_b, hy_f_w1, hy_f_b1, hy_f_w2, hy_f_b2, hy_f_w3, hy_f_b3, hy_f_freq, hy_f_w4, hy_skip, hy_w_out):
    batch, n, d = x.shape
    assert batch == 1 and c.shape[0] == 1 and ctx.shape[0] == 1
    depth = ada_down.shape[0]
    n_mixers = 3
    xs = x.reshape(n, d)
    cs = ctx.reshape(ctx.shape[1], d)

    mla_layers = [i for i in range(depth) if i % n_mixers == 1]
    last_ctx_read = mla_layers[-1] if mla_layers else -1

    s_raw = jnp.concatenate([c.reshape(1, d), c_ctx.reshape(1, d), jnp.zeros((2 * SUBLANES - 2, d), F32)], axis=0)
    mods = _adaln(s_raw, ada_down, ada_up, ada_bias)

    def mod_vecs(i, row):
        return [mods[i, row, m * d:(m + 1) * d] for m in range(N_MOD)]

    ffn_gu, ffn_dn = ffn_w_gate_up, ffn_w_down.astype(BF16)
    sc_in, sc_out = sc_w_in.astype(BF16), sc_w_out.astype(BF16)
    hy_in, hy_out = hy_w_in.astype(BF16), hy_w_out.astype(BF16)
    mla_out = mla_w_out.astype(BF16)
    pend = None
    pend_c = None
    for i in range(depth):
        kind, j = i % n_mixers, i // n_mixers
        ctx_full = i < last_ctx_read
        ctx_keys = i == last_ctx_read
        g = norm_gain[i]
        streams = [(0, xs, pend)]
        if ctx_full or ctx_keys:
            streams.append((1, cs, pend_c))
        hs = {}
        cur = {}
        for row, xv, pd in streams:
            mv = mod_vecs(i, row)
            if pd is None:
                _, hh = _resid_norm_mod(xv, None, _vec_rows(d, g[0], g[0], g[0], mv[0], mv[1]), has_h=True)
            else:
                xv, hh = _resid_norm_mod(xv, pd[0], _vec_rows(d, pd[1], pd[2], g[0], mv[0], mv[1]), has_h=True)
            hs[row], cur[row] = hh, xv

        ys = {}
        if kind == 0:
            w_in, w_out = (sc_in, j), (sc_out, j)
            ys[0] = _short_conv(hs[0], w_in, sc_conv[j], w_out)
            if ctx_full:
                ys[1] = _short_conv(hs[1], w_in, sc_conv[j], w_out)
        elif kind == 1:
            ys[0] = _mla(hs[0], hs[1], mla_w_down[j], mla_q_norm[j], mla_kv_norm[j], mla_w_uq[j], mla_w_ukv[j],
                         (mla_out, j))
            assert not ctx_full
        else:
            hp = ((hy_in, j), hy_conv[j], hy_conv_b[j], hy_f_w1[j], hy_f_b1[j], hy_f_w2[j], hy_f_b2[j],
                  hy_f_w3[j], hy_f_b3[j], hy_f_freq[j], hy_f_w4[j], hy_skip[j], (hy_out, j))
            ys[0] = _hyena(hs[0], *hp)
            if ctx_full:
                ys[1] = _hyena(hs[1], *hp)

        w_gu, w_dn = (ffn_gu, i), (ffn_dn, i)
        new_pend = {0: None, 1: None}
        for row in ys:
            mv = mod_vecs(i, row)
            xv, h2 = _resid_norm_mod(cur[row], ys[row], _vec_rows(d, mv[2], g[1], g[2], mv[3], mv[4]), has_h=True)
            cur[row] = xv
            new_pend[row] = (_ffn(h2, w_gu, w_dn), mv[5], g[3])
        xs, pend = cur[0], new_pend[0]
        if ctx_full:
            cs, pend_c = cur[1], new_pend[1]
        else:
            pend_c = None

    xs, _ = _resid_norm_mod(xs, pend[0], _vec_rows(d, pend[1], pend[2], pend[2], pend[1], pend[1]), has_h=False)
    return xs.reshape(batch, n, d)
```

```python
import functools
import math

import jax
import jax.numpy as jnp
from jax import lax
from jax.experimental import pallas as pl
from jax.experimental.pallas import tpu as pltpu

F32 = jnp.float32
BF16 = jnp.bfloat16

EPS = 1e-6
N_MOD = 6
NOPE_DIM = 128
ROPE_DIM = 64
V_DIM = 128
GRID_W = 64
ROPE_BASE = 10000.0
HY_BANDS = 16
HY_DECAY_TARGET = 1e-2
HY_FAST_DECAY_PCT = 0.3
HY_SLOW_DECAY_PCT = 1.5

LANES = 128
SUBLANES = 8
VMEM_LIMIT_BYTES = 56 * 1024 * 1024
DFT_INNER = 128
HALO_ROWS = 16
MXU_DEPTH = 256

def _tile(dim, pref, align):
    best = None
    t = align
    while t <= min(dim, pref):
        if dim % t == 0:
            best = t
        t += align
    return best if best is not None else dim


def _params(sem):
    return pltpu.CompilerParams(dimension_semantics=sem, vmem_limit_bytes=VMEM_LIMIT_BYTES)


def _split_hi_lo(x):
    hi = x.astype(BF16)
    lo = (x - hi.astype(F32)).astype(BF16)
    return hi, lo


def _dot(a, b):
    return jnp.dot(a, b, preferred_element_type=F32)


def _dot3(a, b):
    ah, al = _split_hi_lo(a)
    bh, bl = _split_hi_lo(b)
    return _dot(ah, bh) + _dot(ah, bl) + _dot(al, bh)


def _rms(x, gain):
    return x * lax.rsqrt(jnp.mean(x * x, axis=-1, keepdims=True) + EPS) * gain


def _silu(x):
    return x * (1.0 / (1.0 + jnp.exp(-x)))


def _mm_body(*refs, n_lhs, n_w, n_extra, n_out, nk, epilogue, lhs_fn):
    refs = list(refs)
    lhs_refs = [refs.pop(0) for _ in range(n_lhs)]
    w_refs = [refs.pop(0) for _ in range(n_w)]
    extra_refs = [refs.pop(0) for _ in range(n_extra)]
    out_refs = [refs.pop(0) for _ in range(n_out)]
    acc_refs = [refs.pop(0) for _ in range(n_w if nk > 1 else 0)]
    if lhs_fn is None:
        a = lhs_refs[0][...].astype(BF16)
    else:
        lhs_scratch = refs.pop(0)

        @pl.when(pl.program_id(1) == 0)
        def _():
            lhs_fn(lhs_refs, lhs_scratch)

        a = lhs_scratch[...]
    dots = [_dot(a, w[...].astype(BF16)) for w in w_refs]
    if nk == 1:
        epilogue(dots, extra_refs, out_refs)
        return
    k = pl.program_id(2)

    @pl.when(k == 0)
    def _():
        for acc, d in zip(acc_refs, dots):
            acc[...] = d

    @pl.when(k > 0)
    def _():
        for acc, d in zip(acc_refs, dots):
            acc[...] += d

    @pl.when(k == nk - 1)
    def _():
        epilogue([acc[...] for acc in acc_refs], extra_refs, out_refs)


def _matmul(a, ws, outs, epilogue, *, extras=(), tm, tn, tk=None, name, lhs_fn=None, lhs_shape=None):
    if lhs_fn is None:
        M, K = a.shape
    else:
        M, K = lhs_shape
    tk = K if tk is None else tk
    nk = K // tk
    assert lhs_fn is None or nk == 1
    n_col_blocks = outs[0][0] // outs[0][1]
    grid = (M // tm, n_col_blocks, nk)
    if lhs_fn is None:
        in_specs = [pl.BlockSpec((tm, tk), lambda i, j, k: (i, k))]
        operands = [a]
    else:
        in_specs = [pl.BlockSpec(bshape, imap) for _, bshape, imap in a]
        operands = [arr for arr, _, _ in a]
    n_lhs = len(operands)
    for w, off in ws:
        if isinstance(w, tuple):
            w, layer = w
            in_specs.append(pl.BlockSpec((None, tk, tn),
                                         functools.partial(lambda i, j, k, o, l: (l, k, j + o), o=off, l=layer)))
        else:
            in_specs.append(pl.BlockSpec((tk, tn), functools.partial(lambda i, j, k, o: (k, j + o), o=off)))
        operands.append(w)
    for arr, bshape, imap in extras:
        in_specs.append(pl.BlockSpec(bshape, imap))
        operands.append(arr)
    out_shape = [jax.ShapeDtypeStruct((M, wt), dt) for wt, _, dt in outs]
    out_specs = [pl.BlockSpec((tm, bw), lambda i, j, k: (i, j)) for _, bw, _ in outs]
    scratch = [pltpu.VMEM((tm, tn), F32) for _ in ws] if nk > 1 else []
    if lhs_fn is not None:
        scratch.append(pltpu.VMEM((tm, K), BF16))
    body = functools.partial(_mm_body, n_lhs=n_lhs, n_w=len(ws), n_extra=len(extras), n_out=len(outs), nk=nk,
                             epilogue=epilogue, lhs_fn=lhs_fn)
    res = pl.pallas_call(
        body, grid=grid, in_specs=in_specs, out_specs=out_specs, out_shape=out_shape,
        scratch_shapes=scratch, name=name,
        compiler_params=_params(("parallel", "arbitrary", "arbitrary")),
    )(*operands)
    return res


def _ep_plain(dots, extras, outs):
    outs[0][...] = dots[0].astype(outs[0].dtype)


def _ep_gate_pair(dots, extras, outs):
    outs[0][...] = dots[0].astype(outs[0].dtype)
    outs[1][...] = (dots[1] * dots[2]).astype(outs[1].dtype)


def _ep_swiglu(dots, extras, outs):
    outs[0][...] = (_silu(dots[0]) * dots[1]).astype(outs[0].dtype)


def _ep_rms(dots, extras, outs):
    outs[0][...] = _rms(dots[0], extras[0][...]).astype(outs[0].dtype)


def _ep_kv_down(dots, extras, outs, *, kv_rank):
    gain_ref, tab_ref = extras
    d = dots[0]
    outs[0][...] = _rms(d[:, :kv_rank], gain_ref[...]).astype(outs[0].dtype)
    p = d[:, kv_rank:] * tab_ref[...]
    outs[1][...] = (p + pltpu.roll(p, ROPE_DIM, axis=1)).astype(outs[1].dtype)


def _ep_k_up(dots, extras, outs):
    kn = dots[0]
    kr = extras[0][...]
    for g in range(kn.shape[1] // NOPE_DIM):
        base = g * (NOPE_DIM + LANES)
        outs[0][:, base:base + NOPE_DIM] = kn[:, g * NOPE_DIM:(g + 1) * NOPE_DIM].astype(outs[0].dtype)
        outs[0][:, base + NOPE_DIM:base + NOPE_DIM + LANES] = kr


def _adaln_body(s_ref, down_ref, up_ref, bias_ref, out_ref, t_ref):
    @pl.when(pl.program_id(1) == 0)
    def _():
        t_ref[...] = _dot3(_silu(s_ref[...]), down_ref[0])

    out_ref[0] = _dot3(t_ref[...], up_ref[0]) + bias_ref[0]


def _adaln(s_raw, ada_down, ada_up, ada_bias):
    depth, d, r = ada_down.shape
    n6 = ada_up.shape[2]
    rows = s_raw.shape[0]
    tn = _tile(n6, 2048, LANES)
    return pl.pallas_call(
        _adaln_body, grid=(depth, n6 // tn),
        in_specs=[pl.BlockSpec((rows, d), lambda i, j: (0, 0)),
                  pl.BlockSpec((1, d, r), lambda i, j: (i, 0, 0)),
                  pl.BlockSpec((1, r, tn), lambda i, j: (i, 0, j)),
                  pl.BlockSpec((1, 1, tn), lambda i, j: (i, 0, j))],
        out_specs=pl.BlockSpec((1, rows, tn), lambda i, j: (i, 0, j)),
        out_shape=jax.ShapeDtypeStruct((depth, rows, n6), F32),
        scratch_shapes=[pltpu.VMEM((rows, r), F32)], name="adaln",
        compiler_params=_params(("arbitrary", "arbitrary")),
    )(s_raw, ada_down, ada_up, ada_bias.reshape(depth, 1, n6))


def _rnm_body(*refs, has_resid, has_h):
    refs = list(refs)
    x_ref = refs.pop(0)
    y_ref = refs.pop(0) if has_resid else None
    vec_ref = refs.pop(0)
    x = x_ref[...]
    if has_resid:
        xo_ref = refs.pop(0)
        x = x + vec_ref[0:1, :] * _rms(y_ref[...].astype(F32), vec_ref[1:2, :])
        xo_ref[...] = x
    if has_h:
        h_ref = refs.pop(0)
        h_ref[...] = (_rms(x, vec_ref[2:3, :]) * (1.0 + vec_ref[4:5, :]) + vec_ref[3:4, :]).astype(h_ref.dtype)


def _resid_norm_mod(x, y, vec, *, has_h, rows=None):
    n, d = x.shape
    n = n if rows is None else rows
    has_resid = y is not None
    tr = _tile(n, 256, 16)
    spec = pl.BlockSpec((tr, d), lambda i: (i, 0))
    in_specs = [spec] + ([spec] if has_resid else []) + [pl.BlockSpec(vec.shape, lambda i: (0, 0))]
    operands = [x] + ([y] if has_resid else []) + [vec]
    out_shape, out_specs = [], []
    if has_resid:
        out_shape.append(jax.ShapeDtypeStruct((n, d), F32))
        out_specs.append(spec)
    if has_h:
        out_shape.append(jax.ShapeDtypeStruct((n, d), BF16))
        out_specs.append(spec)
    res = pl.pallas_call(
        functools.partial(_rnm_body, has_resid=has_resid, has_h=has_h), grid=(n // tr,),
        in_specs=in_specs, out_specs=out_specs, out_shape=out_shape, name="resid_norm_mod",
        compiler_params=_params(("parallel",)),
    )(*operands)
    res = list(res)
    x_new = res.pop(0) if has_resid else None
    h = res.pop(0) if has_h else None
    return x_new, h


def _vec_rows(d, *rows):
    out = [r.reshape(1, d).astype(F32) for r in rows]
    out += [jnp.zeros((1, d), F32)] * (SUBLANES - len(out))
    return jnp.concatenate(out, axis=0)


def _conv3_rows(main, prev_blk, next_blk, w0, w1, w2, is_first, is_last):
    main = main.astype(F32)
    tr = main.shape[0]
    prev_row = jnp.where(is_first, 0.0, prev_blk[HALO_ROWS - 1:HALO_ROWS, :].astype(F32))
    next_row = jnp.where(is_last, 0.0, next_blk[0:1, :].astype(F32))
    up = pltpu.roll(main, 1, axis=0)
    dn = pltpu.roll(main, tr - 1, axis=0)
    rows = lax.broadcasted_iota(jnp.int32, (SUBLANES, main.shape[1]), 0)
    up = jnp.concatenate([jnp.where(rows == 0, prev_row, up[:SUBLANES]), up[SUBLANES:]], axis=0)
    dn = jnp.concatenate([dn[:tr - SUBLANES], jnp.where(rows == SUBLANES - 1, next_row, dn[tr - SUBLANES:])], axis=0)
    return w0 * up + w1 * main + w2 * dn


def _halo_specs(tr, tc, n_rows, col_off):
    per = tr // HALO_ROWS
    last = n_rows // HALO_ROWS - 1
    return [
        pl.BlockSpec((tr, tc), lambda i, j: (i, j + col_off)),
        pl.BlockSpec((HALO_ROWS, tc), lambda i, j: (jnp.maximum(i * per - 1, 0), j + col_off)),
        pl.BlockSpec((HALO_ROWS, tc), lambda i, j: (jnp.minimum((i + 1) * per, last), j + col_off)),
    ]


LHS_CHUNK = 512


def _lhs_sc_gate(refs, scratch):
    b_ref, u_ref, up_ref, un_ref, w_ref = refs
    i = pl.program_id(0)
    first, last = i == 0, i == pl.num_programs(0) - 1
    width = scratch.shape[1]
    cw = min(width, LHS_CHUNK)
    for c in range(0, width, cw):
        sl = slice(c, c + cw)
        conv = _conv3_rows(u_ref[:, sl], up_ref[:, sl], un_ref[:, sl], w_ref[0:1, sl], w_ref[1:2, sl],
                           w_ref[2:3, sl], first, last)
        scratch[:, sl] = (b_ref[:, sl].astype(F32) * conv).astype(scratch.dtype)


def _lhs_hy_gate(refs, scratch):
    x0_ref, y_ref, v_ref, skip_ref = refs
    width = scratch.shape[1]
    cw = min(width, LHS_CHUNK)
    for c in range(0, width, cw):
        sl = slice(c, c + cw)
        yv = y_ref[:, sl] + v_ref[:, sl] * skip_ref[0:1, sl]
        scratch[:, sl] = x0_ref[:, sl] * yv


def _row_halo_specs(tm, d, n_rows):
    per = tm // HALO_ROWS
    last = n_rows // HALO_ROWS - 1
    return [
        ((tm, d), lambda i, j, k: (i, 0)),
        ((HALO_ROWS, d), lambda i, j, k: (jnp.maximum(i * per - 1, 0), 0)),
        ((HALO_ROWS, d), lambda i, j, k: (jnp.minimum((i + 1) * per, last), 0)),
    ]


def _hy_gate_body(*refs):
    z_refs = refs[0:9]
    w_refs = refs[9:12]
    x0_ref, vv_ref = refs[12:14]
    i = pl.program_id(0)
    first, last = i == 0, i == pl.num_programs(0) - 1
    conv = []
    for g in range(3):
        m, p, nx = z_refs[3 * g:3 * g + 3]
        w = w_refs[g]
        conv.append(_conv3_rows(m[...], p[...], nx[...], w[0:1, :], w[1:2, :], w[2:3, :], first, last) + w[3:4, :])
    x0_ref[...] = conv[0].astype(x0_ref.dtype)
    vv_ref[...] = (conv[1] * conv[2]).astype(vv_ref.dtype)


def _hy_gate(z, conv_w, conv_b):
    n, d3 = z.shape
    d = d3 // 3
    tr = _tile(n, 512, 16)
    tc = _tile(d, 512, LANES)
    ncb = d // tc
    w8 = jnp.concatenate([conv_w.astype(F32), conv_b.reshape(1, d3).astype(F32),
                          jnp.zeros((SUBLANES - 4, d3), F32)], axis=0)
    in_specs, operands = [], []
    for g in range(3):
        in_specs += _halo_specs(tr, tc, n, g * ncb)
        operands += [z, z, z]
    for g in range(3):
        in_specs.append(pl.BlockSpec((SUBLANES, tc), functools.partial(lambda i, j, o: (0, j + o), o=g * ncb)))
        operands.append(w8)
    spec = pl.BlockSpec((tr, tc), lambda i, j: (i, j))
    return pl.pallas_call(
        _hy_gate_body, grid=(n // tr, ncb), in_specs=in_specs, out_specs=[spec, spec],
        out_shape=[jax.ShapeDtypeStruct((n, d), BF16)] * 2, name="hy_conv_gate",
        compiler_params=_params(("parallel", "parallel")),
    )(*operands)


def _attn_body(q_ref, k_ref, vt_ref, o_ref, m_ref, l_ref, acc_ref, *, ck):
    ki = pl.program_id(2)

    @pl.when(ki == 0)
    def _():
        m_ref[...] = jnp.full(m_ref.shape, -jnp.inf, F32)
        l_ref[...] = jnp.zeros(l_ref.shape, F32)
        acc_ref[...] = jnp.zeros(acc_ref.shape, F32)

    tq = q_ref.shape[1]
    tk = k_ref.shape[0]
    n_streams = 4 if tq % (4 * MXU_DEPTH) == 0 else 1
    sw = tq // n_streams
    lanes = [slice(h * sw, (h + 1) * sw) for h in range(n_streams)]
    units = tk // MXU_DEPTH if tk % MXU_DEPTH == 0 else 0
    if units:
        nch = max(1, units * MXU_DEPTH // ck)
        sizes = [(units // nch + (1 if c >= nch - units % nch else 0)) * MXU_DEPTH for c in range(nch)]
    else:
        sizes = [min(ck, tk - lo) for lo in range(0, tk, ck)]
    starts = [sum(sizes[:c]) for c in range(len(sizes))]
    bounds = [(lo, lo + sz) for lo, sz in zip(starts, sizes)]

    def scores(b, h):
        return _dot(k_ref[b[0]:b[1], :], q_ref[:, lanes[h]])

    def value_update(acc, pend):
        a_prev, p_prev, b_prev = pend
        return a_prev * acc + _dot(vt_ref[:, b_prev[0]:b_prev[1]], p_prev)

    m = [m_ref[:, ln] for ln in lanes]
    l = [l_ref[:, ln] for ln in lanes]
    acc = [acc_ref[:, ln] for ln in lanes]
    s_next = [scores(bounds[0], h) for h in range(n_streams)]
    pending = [None] * n_streams
    for c, b in enumerate(bounds):
        for h in range(n_streams):
            s = s_next[h]
            if c + 1 < len(bounds):
                s_next[h] = scores(bounds[c + 1], h)
            if pending[h] is not None:
                acc[h] = value_update(acc[h], pending[h])
            m_new = jnp.maximum(m[h], jnp.max(s, axis=0, keepdims=True))
            alpha = jnp.exp2(m[h] - m_new)
            p = jnp.exp2(s - m_new)
            l[h] = alpha * l[h] + jnp.sum(p, axis=0, keepdims=True)
            pending[h] = (alpha, p.astype(BF16), b)
            m[h] = m_new
    for h in range(n_streams):
        acc[h] = value_update(acc[h], pending[h])
        m_ref[:, lanes[h]], l_ref[:, lanes[h]], acc_ref[:, lanes[h]] = m[h], l[h], acc[h]

    @pl.when(ki == pl.num_programs(2) - 1)
    def _():
        for h in range(n_streams):
            o_ref[lanes[h], :] = (acc[h] * (1.0 / l[h])).T.astype(o_ref.dtype)


def _attention(q, k, vt, heads):
    n = q.shape[1]
    nk = k.shape[0]
    qw = q.shape[0] // heads
    unit = MXU_DEPTH if nk % MXU_DEPTH == 0 else LANES
    ck = 6 * MXU_DEPTH
    tq = _tile(n, 2048, LANES)
    tk = _tile(nk, 13 * MXU_DEPTH, unit)
    return pl.pallas_call(
        functools.partial(_attn_body, ck=ck), grid=(heads, n // tq, nk // tk),
        in_specs=[pl.BlockSpec((qw, tq), lambda h, i, j: (h, i)),
                  pl.BlockSpec((tk, qw), lambda h, i, j: (j, h)),
                  pl.BlockSpec((V_DIM, tk), lambda h, i, j: (h, j))],
        out_specs=pl.BlockSpec((tq, V_DIM), lambda h, i, j: (i, h)),
        out_shape=jax.ShapeDtypeStruct((n, heads * V_DIM), BF16),
        scratch_shapes=[pltpu.VMEM((1, tq), F32), pltpu.VMEM((1, tq), F32), pltpu.VMEM((V_DIM, tq), F32)],
        name="mla_flash_attention",
        compiler_params=_params(("parallel", "parallel", "arbitrary")),
    )(q, k, vt)


def _qt_body(w_ref, c_ref, tab_ref, o_ref):
    x = lax.dot_general(w_ref[...], c_ref[...], (((1,), (1,)), ((), ())), preferred_element_type=F32)
    tab = tab_ref[...]
    width = tab.shape[0]
    for g in range(x.shape[0] // width):
        o_ref[g * width:(g + 1) * width, :] = (x[g * width:(g + 1) * width] * tab).astype(o_ref.dtype)


def _q_up_transposed(w_qt, cqn, tab_t):
    hq, r = w_qt.shape
    n = cqn.shape[0]
    qw = tab_t.shape[0]
    th = _tile(hq, 4 * qw, qw)
    tn = _tile(n, 1024, LANES)
    return pl.pallas_call(
        _qt_body, grid=(hq // th, n // tn),
        in_specs=[pl.BlockSpec((th, r), lambda i, j: (i, 0)), pl.BlockSpec((tn, r), lambda i, j: (j, 0)),
                  pl.BlockSpec((qw, tn), lambda i, j: (0, j))],
        out_specs=pl.BlockSpec((th, tn), lambda i, j: (i, j)),
        out_shape=jax.ShapeDtypeStruct((hq, n), BF16), name="mla_q_up_transposed",
        compiler_params=_params(("parallel", "parallel")),
    )(w_qt, cqn, tab_t)


def _vt_body(w_ref, c_ref, o_ref):
    o_ref[...] = lax.dot_general(w_ref[...], c_ref[...], (((1,), (1,)), ((), ())),
                                 preferred_element_type=F32).astype(o_ref.dtype)


def _v_up_transposed(w_vt, ckv):
    hv, r = w_vt.shape
    nk = ckv.shape[0]
    th = _tile(hv, 2 * V_DIM, V_DIM)
    tn = _tile(nk, 13 * MXU_DEPTH, LANES)
    return pl.pallas_call(
        _vt_body, grid=(hv // th, nk // tn),
        in_specs=[pl.BlockSpec((th, r), lambda i, j: (i, 0)), pl.BlockSpec((tn, r), lambda i, j: (j, 0))],
        out_specs=pl.BlockSpec((th, tn), lambda i, j: (i, j)),
        out_shape=jax.ShapeDtypeStruct((hv, nk), BF16), name="mla_v_up_transposed",
        compiler_params=_params(("parallel", "parallel")),
    )(w_vt, ckv)


def _filter_body(z_ref, w1_ref, w2_ref, w3_ref, bf_ref, w4f_ref, w4b_ref, dl_ref, o_ref, h_ref):
    @pl.when(pl.program_id(1) == 0)
    def _():
        bf = bf_ref[...]
        h = jnp.sin(bf[3:4, :] * (_dot3(z_ref[0], w1_ref[...]) + bf[0:1, :]))
        h = jnp.sin(bf[4:5, :] * (_dot3(h, w2_ref[...]) + bf[1:2, :]))
        h_ref[...] = jnp.sin(bf[5:6, :] * (_dot3(h, w3_ref[...]) + bf[2:3, :]))

    z = z_ref[0]
    t = z[:, 0:1]
    sign = z[:, HY_BANDS * 2 + 1:HY_BANDS * 2 + 2]
    h = h_ref[...].astype(BF16)
    half = h.shape[0] // 2
    filt = jnp.concatenate([_dot(h[:half], w4f_ref[...].astype(BF16)), _dot(h[half:], w4b_ref[...].astype(BF16))],
                           axis=0)
    o_ref[...] = (sign * filt * jnp.exp(-t * dl_ref[...])).astype(o_ref.dtype)


def _hyena_filter(n, d, f_w1, f_b1, f_w2, f_b2, f_w3, f_b3, f_freq, f_w4):
    fh = f_w1.shape[1]
    emb = f_w1.shape[0]
    r = jnp.arange(2 * n, dtype=jnp.int32)
    p = jnp.minimum(jnp.where(r < n, r, 2 * n - r), n - 1)
    pf = p.astype(F32)[:, None]
    t = pf / (n - 1)
    w = (2.0 * math.pi / n) * pf
    bands = jnp.linspace(1e-4, HY_BANDS - 1, HY_BANDS, dtype=F32)
    sign = jnp.where(r < n, 1.0, jnp.where(r == n, 0.0, -1.0)).astype(F32)[:, None]
    zw = LANES // 2
    z = jnp.concatenate([t, jnp.cos(bands * w), -jnp.sin(bands * w), sign,
                         jnp.zeros((2 * n, zw - emb - 1), F32)], axis=-1)
    w1p = jnp.concatenate([f_w1.astype(F32), jnp.zeros((zw - emb, fh), F32)], axis=0)
    bf = jnp.concatenate([f_b1.reshape(1, fh), f_b2.reshape(1, fh), f_b3.reshape(1, fh),
                          f_freq.reshape(3, fh), jnp.zeros((2, fh), F32)], axis=0).astype(F32)
    deltas = jnp.abs(jnp.linspace(math.log(HY_DECAY_TARGET) / HY_SLOW_DECAY_PCT,
                                  math.log(HY_DECAY_TARGET) / HY_FAST_DECAY_PCT, d, dtype=F32)).reshape(1, d)
    n2 = DFT_INNER
    n1 = 2 * n // n2
    zt = z.reshape(n1, n2, zw).transpose(1, 0, 2)
    tc = _tile(d, 2048, LANES)
    ncb = d // tc
    const = lambda shape: pl.BlockSpec(shape, lambda i, j: (0, 0))
    w4 = f_w4.astype(F32)
    return pl.pallas_call(
        _filter_body, grid=(n2, ncb),
        in_specs=[pl.BlockSpec((1, n1, zw), lambda i, j: (i, 0, 0)), const((zw, fh)), const((fh, fh)),
                  const((fh, fh)), const((SUBLANES, fh)),
                  pl.BlockSpec((fh, tc), lambda i, j: (0, j)),
                  pl.BlockSpec((fh, tc), lambda i, j: (0, j + ncb)),
                  pl.BlockSpec((1, tc), lambda i, j: (0, j))],
        out_specs=pl.BlockSpec((n1, tc), lambda i, j: (0, i * ncb + j)),
        out_shape=jax.ShapeDtypeStruct((n1, n2 * d), BF16),
        scratch_shapes=[pltpu.VMEM((n1, fh), F32)], name="hyena_filter",
        compiler_params=_params(("parallel", "arbitrary")),
    )(zt, w1p, f_w2.astype(F32), f_w3.astype(F32), bf, w4, w4, deltas)


def _dft_tables(n):
    big_n = 2 * n
    n2 = DFT_INNER
    n1 = big_n // n2
    hk = n1 // 2
    k1 = jnp.arange(hk, dtype=jnp.int32)
    m1 = jnp.arange(n1, dtype=jnp.int32)
    ang_a = (math.pi / n1) * ((m1[None, :] * (2 * k1[:, None] + 1)) % (2 * n1)).astype(F32)
    fwd = jnp.concatenate([jnp.cos(ang_a), -jnp.sin(ang_a)], axis=0)
    inv = (2.0 / big_n) * jnp.concatenate([jnp.cos(ang_a[:, :hk]).T, -jnp.sin(ang_a[:, :hk]).T], axis=1)
    k2 = jnp.arange(n2, dtype=jnp.int32)
    m2 = jnp.arange(n2, dtype=jnp.int32)
    freq = 2 * k1[:, None, None] + 1 + 2 * n1 * k2[None, :, None]
    ang_c = (math.pi / big_n) * ((m2[None, None, :] * freq) % (2 * big_n)).astype(F32)
    gr, gi = jnp.cos(ang_c), -jnp.sin(ang_c)
    mid = jnp.concatenate([jnp.concatenate([gr, -gi], axis=2), jnp.concatenate([gi, gr], axis=2)], axis=1)
    mid_t = jnp.swapaxes(mid, 1, 2)
    return fwd, inv, mid, mid_t


def _dft_rows_body(f_ref, x_ref, o_ref):
    o_ref[...] = _dot(f_ref[...], x_ref[...]).astype(o_ref.dtype)


def _dft_rows(f, x2d, tn):
    r, k = f.shape
    c = x2d.shape[1]
    return pl.pallas_call(
        _dft_rows_body, grid=(c // tn,),
        in_specs=[pl.BlockSpec((r, k), lambda j: (0, 0)), pl.BlockSpec((k, tn), lambda j: (0, j))],
        out_specs=pl.BlockSpec((r, tn), lambda j: (0, j)),
        out_shape=jax.ShapeDtypeStruct((r, c), BF16), name="hyena_dft_outer",
        compiler_params=_params(("parallel",)),
    )(f.astype(BF16), x2d)


def _stack_re_im(ref):
    _, _, n2, tc = ref.shape
    return ref[:, 0].reshape(2 * n2, tc)


def _dft_mid_body(m_ref, a_ref, o_ref):
    n2 = a_ref.shape[2]
    s = _dot(m_ref[0], _stack_re_im(a_ref))
    o_ref[0, 0] = s[:n2].astype(o_ref.dtype)
    o_ref[1, 0] = s[n2:].astype(o_ref.dtype)


def _dft_conv_body(m_ref, t_ref, a_ref, g_ref, o_ref):
    n2 = a_ref.shape[2]
    s = _dot(m_ref[0], _stack_re_im(a_ref))
    sr, si = s[:n2], s[n2:]
    gr, gi = g_ref[0, 0].astype(F32), g_ref[1, 0].astype(F32)
    y = jnp.concatenate([sr * gr - si * gi, sr * gi + si * gr], axis=0).astype(BF16)
    b = _dot(t_ref[0], y)
    o_ref[0, 0] = b[:n2].astype(o_ref.dtype)
    o_ref[1, 0] = b[n2:].astype(o_ref.dtype)


def _dft_mid(mid, a4, spec4=None, mid_t=None):
    _, hk, n2, d = a4.shape
    tc = _tile(d, 4096, LANES)
    mspec = pl.BlockSpec((1, 2 * n2, 2 * n2), lambda k, j: (k, 0, 0))
    dspec = pl.BlockSpec((2, 1, n2, tc), lambda k, j: (0, k, 0, j))
    if spec4 is None:
        body, in_specs, operands, name = _dft_mid_body, [mspec, dspec], [mid.astype(BF16), a4], "hyena_dft_inner"
    else:
        body, in_specs = _dft_conv_body, [mspec, mspec, dspec, dspec]
        operands, name = [mid.astype(BF16), mid_t.astype(BF16), a4, spec4], "hyena_dft_inner_conv"
    return pl.pallas_call(
        body, grid=(hk, d // tc), in_specs=in_specs, out_specs=dspec,
        out_shape=jax.ShapeDtypeStruct(a4.shape, BF16), name=name,
        compiler_params=_params(("parallel", "arbitrary")),
    )(*operands)


def _row_tile(m):
    return _tile(m, 1024, 16)


def _short_conv(h, w_in, w_conv, w_out):
    m, d = h.shape
    tm = _row_tile(m)
    tn_in = _tile(d, 256, LANES)
    ncb = d // tn_in
    b, u = _matmul(h, [(w_in, 0), (w_in, ncb), (w_in, 2 * ncb)], [(d, tn_in, BF16), (d, tn_in, BF16)],
                   _ep_gate_pair, tm=tm, tn=tn_in, name="sc_in_proj")
    tn = _tile(d, 512, LANES)
    tmo = _tile(m, 512, HALO_ROWS)
    w8 = jnp.concatenate([w_conv.astype(F32), jnp.zeros((SUBLANES - 3, d), F32)], axis=0)
    main, prev, nxt = _row_halo_specs(tmo, d, m)
    lhs = [(b, *main), (u, *main), (u, *prev), (u, *nxt), (w8, (SUBLANES, d), lambda i, j, k: (0, 0))]
    return _matmul(lhs, [(w_out, 0)], [(d, tn, BF16)], _ep_plain, tm=tmo, tn=tn, name="sc_out_proj",
                   lhs_fn=_lhs_sc_gate, lhs_shape=(m, d))[0]


def _ffn(h2, w_gate_up, w_down):
    m, d = h2.shape
    f = w_down[0].shape[1]
    tm = _row_tile(m)
    tf = _tile(f, 512, LANES)
    a = _matmul(h2, [(w_gate_up, 0), (w_gate_up, f // tf)], [(f, tf, BF16)], _ep_swiglu,
                tm=tm, tn=tf, name="ffn_gate_up")[0]
    tn = _tile(d, 512, LANES)
    return _matmul(a, [(w_down, 0)], [(d, tn, BF16)], _ep_plain, tm=_tile(m, 512, 16), tn=tn, name="ffn_down")[0]


def _rope_tables(n, n_ctx, scale):
    rows = n // GRID_W
    row = jnp.repeat(jnp.arange(rows, dtype=F32), GRID_W)
    col = jnp.tile(jnp.arange(GRID_W, dtype=F32), rows)
    axis_dim = ROPE_DIM // 2
    inv = ROPE_BASE ** (-jnp.arange(0, axis_dim, 2, dtype=F32) / axis_dim)
    ang_r, ang_c = row[:, None] * inv, col[:, None] * inv
    cos = jnp.concatenate([jnp.cos(ang_r), jnp.cos(ang_c)], axis=1)
    sin = jnp.concatenate([jnp.sin(ang_r), jnp.sin(ang_c)], axis=1)
    rot = jnp.concatenate([cos, cos, -sin, sin], axis=1)
    tab_q = scale * jnp.concatenate([jnp.ones((n, NOPE_DIM), F32), rot], axis=1)
    ctx_rot = jnp.concatenate([jnp.ones((n_ctx, ROPE_DIM), F32), jnp.zeros((n_ctx, ROPE_DIM), F32)], axis=1)
    return tab_q, rot, ctx_rot


def _mla(h, hc, w_down, q_norm, kv_norm, w_uq, w_ukv, w_out):
    n, d = h.shape
    n_ctx = hc.shape[0]
    q_rank, kv_rank = w_uq.shape[0], w_ukv.shape[0]
    heads = w_uq.shape[1] // (NOPE_DIM + ROPE_DIM)
    scale = (NOPE_DIM + ROPE_DIM) ** -0.5 * math.log2(math.e)
    tab_q, tab_k, tab_kc = _rope_tables(n, n_ctx, scale)

    def rope_cols(w):
        qd = ROPE_DIM // 4
        a = jnp.concatenate([w[..., 0:qd], w[..., 2 * qd:3 * qd]], axis=-1)
        b = jnp.concatenate([w[..., qd:2 * qd], w[..., 3 * qd:4 * qd]], axis=-1)
        return jnp.concatenate([a, b, b, a], axis=-1)

    w_dq = w_down[:, :q_rank].astype(BF16)
    w_dkv = jnp.concatenate([w_down[:, q_rank:q_rank + kv_rank], rope_cols(w_down[:, q_rank + kv_rank:])],
                            axis=1).astype(BF16)
    wq3 = w_uq.reshape(q_rank, heads, NOPE_DIM + ROPE_DIM)
    w_uq_p = jnp.concatenate([wq3[:, :, :NOPE_DIM], rope_cols(wq3[:, :, NOPE_DIM:])], axis=2)
    qw = NOPE_DIM + 2 * ROPE_DIM
    w_uq_p = w_uq_p.reshape(q_rank, heads * qw).astype(BF16)
    wkv3 = w_ukv.reshape(kv_rank, heads, NOPE_DIM + V_DIM)
    w_kn = wkv3[:, :, :NOPE_DIM].reshape(kv_rank, heads * NOPE_DIM).astype(BF16)
    w_vt = wkv3[:, :, NOPE_DIM:].reshape(kv_rank, heads * V_DIM).T.astype(BF16)

    tm = _row_tile(n)
    cqn = _matmul(h, [(w_dq, 0)], [(q_rank, q_rank, BF16)], _ep_rms,
                  extras=[(q_norm.reshape(1, q_rank).astype(F32), (1, q_rank), lambda i, j, k: (0, 0))],
                  tm=tm, tn=q_rank, name="mla_q_down")[0]

    def kv_down(hh, tab, name):
        m = hh.shape[0]
        tmk = _row_tile(m)
        wd = kv_rank + 2 * ROPE_DIM
        return _matmul(hh, [(w_dkv, 0)], [(kv_rank, kv_rank, BF16), (LANES, LANES, BF16)],
                       functools.partial(_ep_kv_down, kv_rank=kv_rank),
                       extras=[(kv_norm.reshape(1, kv_rank).astype(F32), (1, kv_rank), lambda i, j, k: (0, 0)),
                               (tab, (tmk, 2 * ROPE_DIM), lambda i, j, k: (i, 0))],
                       tm=tmk, tn=wd, name=name)

    ckv, kr = kv_down(h, tab_k, "mla_kv_down")
    ckv_c, kr_c = kv_down(hc, tab_kc, "mla_kv_down_ctx")
    ckv = jnp.concatenate([ckv, ckv_c], axis=0)
    kr = jnp.concatenate([kr, kr_c], axis=0)
    nk = n + n_ctx

    gq = 2 if heads % 2 == 0 else 1
    q = _q_up_transposed(w_uq_p.T, cqn, tab_q.T)
    tmk = _tile(nk, 1664, 16)
    k = _matmul(ckv, [(w_kn, 0)], [(heads * qw, gq * qw, BF16)], _ep_k_up,
                extras=[(kr, (tmk, LANES), lambda i, j, k: (i, 0))], tm=tmk, tn=gq * NOPE_DIM, name="mla_k_up")[0]
    vt = _v_up_transposed(w_vt, ckv)
    o = _attention(q, k, vt, heads)
    tn = _tile(d, 512, LANES)
    return _matmul(o, [(w_out, 0)], [(d, tn, BF16)], _ep_plain, tm=tm, tn=tn, name="mla_out_proj")[0]


def _hyena(h, w_in, conv_w, conv_b, f_w1, f_b1, f_w2, f_b2, f_w3, f_b3, f_freq, f_w4, skip, w_out):
    n, d = h.shape
    tm = _row_tile(n)
    tn = _tile(d, 512, LANES)
    z = _matmul(h, [(w_in, 0)], [(3 * d, tn, BF16)], _ep_plain, tm=tm, tn=tn, name="hy_in_proj")[0]
    x0, vv = _hy_gate(z, conv_w, conv_b)
    n2 = DFT_INNER
    n1 = 2 * n // n2
    hk = n1 // 2
    fwd, inv, mid, mid_t = _dft_tables(n)
    g = _hyena_filter(n, d, f_w1, f_b1, f_w2, f_b2, f_w3, f_b3, f_freq, f_w4)
    tcol = _tile(n2 * d, 4096, LANES)
    ga = _dft_rows(fwd, g, tcol)
    spec = _dft_mid(mid, ga.reshape(2, hk, n2, d))
    va = _dft_rows(fwd[:, :hk], vv.reshape(hk, n2 * d), tcol)
    vb = _dft_mid(mid, va.reshape(2, hk, n2, d), spec, mid_t)
    y = _dft_rows(inv, vb.reshape(n1, n2 * d), tcol).reshape(n, d)
    tmo = _tile(n, 512, 16)
    row = ((tmo, d), lambda i, j, k: (i, 0))
    skip_rows = jnp.broadcast_to(skip.reshape(1, d).astype(BF16), (HALO_ROWS, d))
    lhs = [(x0, *row), (y, *row), (vv, *row), (skip_rows, (HALO_ROWS, d), lambda i, j, k: (0, 0))]
    return _matmul(lhs, [(w_out, 0)], [(d, tn, BF16)], _ep_plain, tm=tmo, tn=tn, name="hy_out_proj",
                   lhs_fn=_lhs_hy_gate, lhs_shape=(n, d))[0]


def kernel(x, c, ctx, c_ctx, ada_down, ada_up, ada_bias, norm_gain, ffn_w_gate_up, ffn_w_down, sc_w_in, sc_conv, sc_w_out, mla_w_down, mla_q_norm, mla_kv_norm, mla_w_uq, mla_w_ukv, mla_w_out, hy_w_in, hy_conv, hy_conv_b, hy_f_w1, hy_f_b1, hy_f_w2, hy_f_b2, hy_f_w3, hy_f_b3, hy_f_freq, hy_f_w4, hy_skip, hy_w_out):
    batch, n, d = x.shape
    assert batch == 1 and c.shape[0] == 1 and ctx.shape[0] == 1
    depth = ada_down.shape[0]
    n_mixers = 3
    xs = x.reshape(n, d)
    cs = ctx.reshape(ctx.shape[1], d)

    mla_layers = [i for i in range(depth) if i % n_mixers == 1]
    last_ctx_read = mla_layers[-1] if mla_layers else -1

    s_raw = jnp.concatenate([c.reshape(1, d), c_ctx.reshape(1, d), jnp.zeros((2 * SUBLANES - 2, d), F32)], axis=0)
    mods = _adaln(s_raw, ada_down, ada_up, ada_bias)

    def mod_vecs(i, row):
        return [mods[i, row, m * d:(m + 1) * d] for m in range(N_MOD)]

    ffn_gu, ffn_dn = ffn_w_gate_up, ffn_w_down.astype(BF16)
    sc_in, sc_out = sc_w_in.astype(BF16), sc_w_out.astype(BF16)
    hy_in, hy_out = hy_w_in, hy_w_out.astype(BF16)
    mla_out = mla_w_out.astype(BF16)
    pend = None
    pend_c = None
    for i in range(depth):
        kind, j = i % n_mixers, i // n_mixers
        ctx_full = i < last_ctx_read
        ctx_keys = i == last_ctx_read
        g = norm_gain[i]
        streams = [(0, xs, pend)]
        if ctx_full or ctx_keys:
            streams.append((1, cs, pend_c))
        hs = {}
        cur = {}
        for row, xv, pd in streams:
            mv = mod_vecs(i, row)
            if pd is None:
                _, hh = _resid_norm_mod(xv, None, _vec_rows(d, g[0], g[0], g[0], mv[0], mv[1]), has_h=True)
            else:
                xv, hh = _resid_norm_mod(xv, pd[0], _vec_rows(d, pd[1], pd[2], g[0], mv[0], mv[1]), has_h=True)
            hs[row], cur[row] = hh, xv

        ys = {}
        if kind == 0:
            w_in, w_out = (sc_in, j), (sc_out, j)
            ys[0] = _short_conv(hs[0], w_in, sc_conv[j], w_out)
            if ctx_full:
                ys[1] = _short_conv(hs[1], w_in, sc_conv[j], w_out)
        elif kind == 1:
            ys[0] = _mla(hs[0], hs[1], mla_w_down[j], mla_q_norm[j], mla_kv_norm[j], mla_w_uq[j], mla_w_ukv[j],
                         (mla_out, j))
            assert not ctx_full
        else:
            hp = ((hy_in, j), hy_conv[j], hy_conv_b[j], hy_f_w1[j], hy_f_b1[j], hy_f_w2[j], hy_f_b2[j],
                  hy_f_w3[j], hy_f_b3[j], hy_f_freq[j], hy_f_w4[j], hy_skip[j], (hy_out, j))
            ys[0] = _hyena(hs[0], *hp)
            if ctx_full:
                ys[1] = _hyena(hs[1], *hp)

        w_gu, w_dn = (ffn_gu, i), (ffn_dn, i)
        new_pend = {0: None, 1: None}
        for row in ys:
            mv = mod_vecs(i, row)
            xv, h2 = _resid_norm_mod(cur[row], ys[row], _vec_rows(d, mv[2], g[1], g[2], mv[3], mv[4]), has_h=True)
            cur[row] = xv
            new_pend[row] = (_ffn(h2, w_gu, w_dn), mv[5], g[3])
        xs, pend = cur[0], new_pend[0]
        if ctx_full:
            cs, pend_c = cur[1], new_pend[1]
        else:
            pend_c = None

    xs, _ = _resid_norm_mod(xs, pend[0], _vec_rows(d, pend[1], pend[2], pend[2], pend[1], pend[1]), has_h=False)
    return xs.reshape(batch, n, d)
```

```python
import functools
import math

import jax
import jax.numpy as jnp
from jax import lax
from jax.experimental import pallas as pl
from jax.experimental.pallas import tpu as pltpu

F32 = jnp.float32
BF16 = jnp.bfloat16

EPS = 1e-6
N_MOD = 6
NOPE_DIM = 128
ROPE_DIM = 64
V_DIM = 128
GRID_W = 64
ROPE_BASE = 10000.0
HY_BANDS = 16
HY_DECAY_TARGET = 1e-2
HY_FAST_DECAY_PCT = 0.3
HY_SLOW_DECAY_PCT = 1.5

LANES = 128
SUBLANES = 8
VMEM_LIMIT_BYTES = 56 * 1024 * 1024
DFT_INNER = 128
HALO_ROWS = 16
MXU_DEPTH = 256

def _tile(dim, pref, align):
    best = None
    t = align
    while t <= min(dim, pref):
        if dim % t == 0:
            best = t
        t += align
    return best if best is not None else dim


def _params(sem):
    return pltpu.CompilerParams(dimension_semantics=sem, vmem_limit_bytes=VMEM_LIMIT_BYTES)


def _split_hi_lo(x):
    hi = x.astype(BF16)
    lo = (x - hi.astype(F32)).astype(BF16)
    return hi, lo


def _dot(a, b):
    return jnp.dot(a, b, preferred_element_type=F32)


def _dot3(a, b):
    ah, al = _split_hi_lo(a)
    bh, bl = _split_hi_lo(b)
    return _dot(ah, bh) + _dot(ah, bl) + _dot(al, bh)


def _rms(x, gain):
    return x * lax.rsqrt(jnp.mean(x * x, axis=-1, keepdims=True) + EPS) * gain


def _silu(x):
    return x * (1.0 / (1.0 + jnp.exp(-x)))


def _mm_body(*refs, n_lhs, n_w, n_extra, n_out, nk, epilogue, lhs_fn):
    refs = list(refs)
    lhs_refs = [refs.pop(0) for _ in range(n_lhs)]
    w_refs = [refs.pop(0) for _ in range(n_w)]
    extra_refs = [refs.pop(0) for _ in range(n_extra)]
    out_refs = [refs.pop(0) for _ in range(n_out)]
    acc_refs = [refs.pop(0) for _ in range(n_w if nk > 1 else 0)]
    if lhs_fn is None:
        a = lhs_refs[0][...].astype(BF16)
    else:
        lhs_scratch = refs.pop(0)

        @pl.when(pl.program_id(1) == 0)
        def _():
            lhs_fn(lhs_refs, lhs_scratch)

        a = lhs_scratch[...]
    dots = [_dot(a, w[...].astype(BF16)) for w in w_refs]
    if nk == 1:
        epilogue(dots, extra_refs, out_refs)
        return
    k = pl.program_id(2)

    @pl.when(k == 0)
    def _():
        for acc, d in zip(acc_refs, dots):
            acc[...] = d

    @pl.when(k > 0)
    def _():
        for acc, d in zip(acc_refs, dots):
            acc[...] += d

    @pl.when(k == nk - 1)
    def _():
        epilogue([acc[...] for acc in acc_refs], extra_refs, out_refs)


def _matmul(a, ws, outs, epilogue, *, extras=(), tm, tn, tk=None, name, lhs_fn=None, lhs_shape=None):
    if lhs_fn is None:
        M, K = a.shape
    else:
        M, K = lhs_shape
    tk = K if tk is None else tk
    nk = K // tk
    assert lhs_fn is None or nk == 1
    n_col_blocks = outs[0][0] // outs[0][1]
    grid = (M // tm, n_col_blocks, nk)
    if lhs_fn is None:
        in_specs = [pl.BlockSpec((tm, tk), lambda i, j, k: (i, k))]
        operands = [a]
    else:
        in_specs = [pl.BlockSpec(bshape, imap) for _, bshape, imap in a]
        operands = [arr for arr, _, _ in a]
    n_lhs = len(operands)
    for w, off in ws:
        if isinstance(w, tuple):
            w, layer = w
            in_specs.append(pl.BlockSpec((None, tk, tn),
                                         functools.partial(lambda i, j, k, o, l: (l, k, j + o), o=off, l=layer)))
        else:
            in_specs.append(pl.BlockSpec((tk, tn), functools.partial(lambda i, j, k, o: (k, j + o), o=off)))
        operands.append(w)
    for arr, bshape, imap in extras:
        in_specs.append(pl.BlockSpec(bshape, imap))
        operands.append(arr)
    out_shape = [jax.ShapeDtypeStruct((M, wt), dt) for wt, _, dt in outs]
    out_specs = [pl.BlockSpec((tm, bw), lambda i, j, k: (i, j)) for _, bw, _ in outs]
    scratch = [pltpu.VMEM((tm, tn), F32) for _ in ws] if nk > 1 else []
    if lhs_fn is not None:
        scratch.append(pltpu.VMEM((tm, K), BF16))
    body = functools.partial(_mm_body, n_lhs=n_lhs, n_w=len(ws), n_extra=len(extras), n_out=len(outs), nk=nk,
                             epilogue=epilogue, lhs_fn=lhs_fn)
    res = pl.pallas_call(
        body, grid=grid, in_specs=in_specs, out_specs=out_specs, out_shape=out_shape,
        scratch_shapes=scratch, name=name,
        compiler_params=_params(("parallel", "arbitrary", "arbitrary")),
    )(*operands)
    return res


def _ep_plain(dots, extras, outs):
    outs[0][...] = dots[0].astype(outs[0].dtype)


def _ep_gate_pair(dots, extras, outs):
    outs[0][...] = dots[0].astype(outs[0].dtype)
    outs[1][...] = (dots[1] * dots[2]).astype(outs[1].dtype)


def _ep_swiglu(dots, extras, outs):
    outs[0][...] = (_silu(dots[0]) * dots[1]).astype(outs[0].dtype)


def _ep_rms(dots, extras, outs):
    outs[0][...] = _rms(dots[0], extras[0][...]).astype(outs[0].dtype)


def _ep_kv_down(dots, extras, outs, *, kv_rank):
    gain_ref, tab_ref = extras
    d = dots[0]
    outs[0][...] = _rms(d[:, :kv_rank], gain_ref[...]).astype(outs[0].dtype)
    p = d[:, kv_rank:] * tab_ref[...]
    outs[1][...] = (p + pltpu.roll(p, ROPE_DIM, axis=1)).astype(outs[1].dtype)


def _ep_k_up(dots, extras, outs):
    kn = dots[0]
    kr = extras[0][...]
    for g in range(kn.shape[1] // NOPE_DIM):
        base = g * (NOPE_DIM + LANES)
        outs[0][:, base:base + NOPE_DIM] = kn[:, g * NOPE_DIM:(g + 1) * NOPE_DIM].astype(outs[0].dtype)
        outs[0][:, base + NOPE_DIM:base + NOPE_DIM + LANES] = kr


def _adaln_body(s_ref, down_ref, up_ref, bias_ref, out_ref, t_ref):
    @pl.when(pl.program_id(1) == 0)
    def _():
        t_ref[...] = _dot3(_silu(s_ref[...]), down_ref[0])

    out_ref[0] = _dot3(t_ref[...], up_ref[0]) + bias_ref[0]


def _adaln(s_raw, ada_down, ada_up, ada_bias):
    depth, d, r = ada_down.shape
    n6 = ada_up.shape[2]
    rows = s_raw.shape[0]
    tn = _tile(n6, 2048, LANES)
    return pl.pallas_call(
        _adaln_body, grid=(depth, n6 // tn),
        in_specs=[pl.BlockSpec((rows, d), lambda i, j: (0, 0)),
                  pl.BlockSpec((1, d, r), lambda i, j: (i, 0, 0)),
                  pl.BlockSpec((1, r, tn), lambda i, j: (i, 0, j)),
                  pl.BlockSpec((1, 1, tn), lambda i, j: (i, 0, j))],
        out_specs=pl.BlockSpec((1, rows, tn), lambda i, j: (i, 0, j)),
        out_shape=jax.ShapeDtypeStruct((depth, rows, n6), F32),
        scratch_shapes=[pltpu.VMEM((rows, r), F32)], name="adaln",
        compiler_params=_params(("arbitrary", "arbitrary")),
    )(s_raw, ada_down, ada_up, ada_bias.reshape(depth, 1, n6))


def _rnm_body(*refs, has_resid, has_h):
    refs = list(refs)
    x_ref = refs.pop(0)
    y_ref = refs.pop(0) if has_resid else None
    vec_ref = refs.pop(0)
    x = x_ref[...]
    if has_resid:
        xo_ref = refs.pop(0)
        x = x + vec_ref[0:1, :] * _rms(y_ref[...].astype(F32), vec_ref[1:2, :])
        xo_ref[...] = x
    if has_h:
        h_ref = refs.pop(0)
        h_ref[...] = (_rms(x, vec_ref[2:3, :]) * (1.0 + vec_ref[4:5, :]) + vec_ref[3:4, :]).astype(h_ref.dtype)


def _resid_norm_mod(x, y, vec, *, has_h, rows=None):
    n, d = x.shape
    n = n if rows is None else rows
    has_resid = y is not None
    tr = _tile(n, 256, 16)
    spec = pl.BlockSpec((tr, d), lambda i: (i, 0))
    in_specs = [spec] + ([spec] if has_resid else []) + [pl.BlockSpec(vec.shape, lambda i: (0, 0))]
    operands = [x] + ([y] if has_resid else []) + [vec]
    out_shape, out_specs = [], []
    if has_resid:
        out_shape.append(jax.ShapeDtypeStruct((n, d), F32))
        out_specs.append(spec)
    if has_h:
        out_shape.append(jax.ShapeDtypeStruct((n, d), BF16))
        out_specs.append(spec)
    res = pl.pallas_call(
        functools.partial(_rnm_body, has_resid=has_resid, has_h=has_h), grid=(n // tr,),
        in_specs=in_specs, out_specs=out_specs, out_shape=out_shape, name="resid_norm_mod",
        compiler_params=_params(("parallel",)),
    )(*operands)
    res = list(res)
    x_new = res.pop(0) if has_resid else None
    h = res.pop(0) if has_h else None
    return x_new, h


def _vec_rows(d, *rows):
    out = [r.reshape(1, d).astype(F32) for r in rows]
    out += [jnp.zeros((1, d), F32)] * (SUBLANES - len(out))
    return jnp.concatenate(out, axis=0)


def _conv3_rows(main, prev_blk, next_blk, w0, w1, w2, is_first, is_last):
    main = main.astype(F32)
    tr = main.shape[0]
    prev_row = jnp.where(is_first, 0.0, prev_blk[HALO_ROWS - 1:HALO_ROWS, :].astype(F32))
    next_row = jnp.where(is_last, 0.0, next_blk[0:1, :].astype(F32))
    up = pltpu.roll(main, 1, axis=0)
    dn = pltpu.roll(main, tr - 1, axis=0)
    rows = lax.broadcasted_iota(jnp.int32, (SUBLANES, main.shape[1]), 0)
    up = jnp.concatenate([jnp.where(rows == 0, prev_row, up[:SUBLANES]), up[SUBLANES:]], axis=0)
    dn = jnp.concatenate([dn[:tr - SUBLANES], jnp.where(rows == SUBLANES - 1, next_row, dn[tr - SUBLANES:])], axis=0)
    return w0 * up + w1 * main + w2 * dn


def _halo_specs(tr, tc, n_rows, col_off):
    per = tr // HALO_ROWS
    last = n_rows // HALO_ROWS - 1
    return [
        pl.BlockSpec((tr, tc), lambda i, j: (i, j + col_off)),
        pl.BlockSpec((HALO_ROWS, tc), lambda i, j: (jnp.maximum(i * per - 1, 0), j + col_off)),
        pl.BlockSpec((HALO_ROWS, tc), lambda i, j: (jnp.minimum((i + 1) * per, last), j + col_off)),
    ]


LHS_CHUNK = 512


def _lhs_sc_gate(refs, scratch):
    b_ref, u_ref, up_ref, un_ref, w_ref = refs
    i = pl.program_id(0)
    first, last = i == 0, i == pl.num_programs(0) - 1
    width = scratch.shape[1]
    cw = min(width, LHS_CHUNK)
    for c in range(0, width, cw):
        sl = slice(c, c + cw)
        conv = _conv3_rows(u_ref[:, sl], up_ref[:, sl], un_ref[:, sl], w_ref[0:1, sl], w_ref[1:2, sl],
                           w_ref[2:3, sl], first, last)
        scratch[:, sl] = (b_ref[:, sl].astype(F32) * conv).astype(scratch.dtype)


def _lhs_hy_gate(refs, scratch):
    x0_ref, y_ref, v_ref, skip_ref = refs
    width = scratch.shape[1]
    cw = min(width, LHS_CHUNK)
    for c in range(0, width, cw):
        sl = slice(c, c + cw)
        yv = y_ref[:, sl] + v_ref[:, sl] * skip_ref[0:1, sl]
        scratch[:, sl] = x0_ref[:, sl] * yv


def _row_halo_specs(tm, d, n_rows):
    per = tm // HALO_ROWS
    last = n_rows // HALO_ROWS - 1
    return [
        ((tm, d), lambda i, j, k: (i, 0)),
        ((HALO_ROWS, d), lambda i, j, k: (jnp.maximum(i * per - 1, 0), 0)),
        ((HALO_ROWS, d), lambda i, j, k: (jnp.minimum((i + 1) * per, last), 0)),
    ]


def _hy_gate_body(*refs):
    z_refs = refs[0:9]
    w_refs = refs[9:12]
    x0_ref, vv_ref = refs[12:14]
    i = pl.program_id(0)
    first, last = i == 0, i == pl.num_programs(0) - 1
    conv = []
    for g in range(3):
        m, p, nx = z_refs[3 * g:3 * g + 3]
        w = w_refs[g]
        conv.append(_conv3_rows(m[...], p[...], nx[...], w[0:1, :], w[1:2, :], w[2:3, :], first, last) + w[3:4, :])
    x0_ref[...] = conv[0].astype(x0_ref.dtype)
    vv_ref[...] = (conv[1] * conv[2]).astype(vv_ref.dtype)


def _hy_gate(z, conv_w, conv_b):
    n, d3 = z.shape
    d = d3 // 3
    tr = _tile(n, 512, 16)
    tc = _tile(d, 512, LANES)
    ncb = d // tc
    w8 = jnp.concatenate([conv_w.astype(F32), conv_b.reshape(1, d3).astype(F32),
                          jnp.zeros((SUBLANES - 4, d3), F32)], axis=0)
    in_specs, operands = [], []
    for g in range(3):
        in_specs += _halo_specs(tr, tc, n, g * ncb)
        operands += [z, z, z]
    for g in range(3):
        in_specs.append(pl.BlockSpec((SUBLANES, tc), functools.partial(lambda i, j, o: (0, j + o), o=g * ncb)))
        operands.append(w8)
    spec = pl.BlockSpec((tr, tc), lambda i, j: (i, j))
    return pl.pallas_call(
        _hy_gate_body, grid=(n // tr, ncb), in_specs=in_specs, out_specs=[spec, spec],
        out_shape=[jax.ShapeDtypeStruct((n, d), BF16)] * 2, name="hy_conv_gate",
        compiler_params=_params(("parallel", "parallel")),
    )(*operands)


def _attn_body(q_ref, k_ref, vt_ref, o_ref, m_ref, l_ref, acc_ref, *, ck):
    ki = pl.program_id(2)

    @pl.when(ki == 0)
    def _():
        m_ref[...] = jnp.full(m_ref.shape, -jnp.inf, F32)
        l_ref[...] = jnp.zeros(l_ref.shape, F32)
        acc_ref[...] = jnp.zeros(acc_ref.shape, F32)

    tq = q_ref.shape[1]
    tk = k_ref.shape[0]
    n_streams = 4 if tq % (4 * MXU_DEPTH) == 0 else 1
    sw = tq // n_streams
    lanes = [slice(h * sw, (h + 1) * sw) for h in range(n_streams)]
    units = tk // MXU_DEPTH if tk % MXU_DEPTH == 0 else 0
    if units:
        nch = max(1, units * MXU_DEPTH // ck)
        sizes = [(units // nch + (1 if c >= nch - units % nch else 0)) * MXU_DEPTH for c in range(nch)]
    else:
        sizes = [min(ck, tk - lo) for lo in range(0, tk, ck)]
    starts = [sum(sizes[:c]) for c in range(len(sizes))]
    bounds = [(lo, lo + sz) for lo, sz in zip(starts, sizes)]

    def scores(b, h):
        return _dot(k_ref[b[0]:b[1], :], q_ref[:, lanes[h]])

    def value_update(acc, pend):
        a_prev, p_prev, b_prev = pend
        return a_prev * acc + _dot(vt_ref[:, b_prev[0]:b_prev[1]], p_prev)

    m = [m_ref[:, ln] for ln in lanes]
    l = [l_ref[:, ln] for ln in lanes]
    acc = [acc_ref[:, ln] for ln in lanes]
    s_next = [scores(bounds[0], h) for h in range(n_streams)]
    pending = [None] * n_streams
    for c, b in enumerate(bounds):
        for h in range(n_streams):
            s = s_next[h]
            if c + 1 < len(bounds):
                s_next[h] = scores(bounds[c + 1], h)
            if pending[h] is not None:
                acc[h] = value_update(acc[h], pending[h])
            m_new = jnp.maximum(m[h], jnp.max(s, axis=0, keepdims=True))
            alpha = jnp.exp2(m[h] - m_new)
            p = jnp.exp2(s - m_new)
            l[h] = alpha * l[h] + jnp.sum(p, axis=0, keepdims=True)
            pending[h] = (alpha, p.astype(BF16), b)
            m[h] = m_new
    for h in range(n_streams):
        acc[h] = value_update(acc[h], pending[h])
        m_ref[:, lanes[h]], l_ref[:, lanes[h]], acc_ref[:, lanes[h]] = m[h], l[h], acc[h]

    @pl.when(ki == pl.num_programs(2) - 1)
    def _():
        for h in range(n_streams):
            o_ref[lanes[h], :] = (acc[h] * (1.0 / l[h])).T.astype(o_ref.dtype)


def _attention(q, k, vt, heads):
    n = q.shape[1]
    nk = k.shape[0]
    qw = q.shape[0] // heads
    unit = MXU_DEPTH if nk % MXU_DEPTH == 0 else LANES
    ck = 6 * MXU_DEPTH
    tq = _tile(n, 2048, LANES)
    tk = _tile(nk, 13 * MXU_DEPTH, unit)
    return pl.pallas_call(
        functools.partial(_attn_body, ck=ck), grid=(heads, n // tq, nk // tk),
        in_specs=[pl.BlockSpec((qw, tq), lambda h, i, j: (h, i)),
                  pl.BlockSpec((tk, qw), lambda h, i, j: (j, h)),
                  pl.BlockSpec((V_DIM, tk), lambda h, i, j: (h, j))],
        out_specs=pl.BlockSpec((tq, V_DIM), lambda h, i, j: (i, h)),
        out_shape=jax.ShapeDtypeStruct((n, heads * V_DIM), BF16),
        scratch_shapes=[pltpu.VMEM((1, tq), F32), pltpu.VMEM((1, tq), F32), pltpu.VMEM((V_DIM, tq), F32)],
        name="mla_flash_attention",
        compiler_params=_params(("parallel", "parallel", "arbitrary")),
    )(q, k, vt)


def _qt_body(w_ref, c_ref, tab_ref, o_ref):
    x = lax.dot_general(w_ref[...], c_ref[...], (((1,), (1,)), ((), ())), preferred_element_type=F32)
    tab = tab_ref[...]
    width = tab.shape[0]
    for g in range(x.shape[0] // width):
        o_ref[g * width:(g + 1) * width, :] = (x[g * width:(g + 1) * width] * tab).astype(o_ref.dtype)


def _q_up_transposed(w_qt, cqn, tab_t):
    hq, r = w_qt.shape
    n = cqn.shape[0]
    qw = tab_t.shape[0]
    th = _tile(hq, 4 * qw, qw)
    tn = _tile(n, 1024, LANES)
    return pl.pallas_call(
        _qt_body, grid=(hq // th, n // tn),
        in_specs=[pl.BlockSpec((th, r), lambda i, j: (i, 0)), pl.BlockSpec((tn, r), lambda i, j: (j, 0)),
                  pl.BlockSpec((qw, tn), lambda i, j: (0, j))],
        out_specs=pl.BlockSpec((th, tn), lambda i, j: (i, j)),
        out_shape=jax.ShapeDtypeStruct((hq, n), BF16), name="mla_q_up_transposed",
        compiler_params=_params(("parallel", "parallel")),
    )(w_qt, cqn, tab_t)


def _vt_body(w_ref, c_ref, o_ref):
    o_ref[...] = lax.dot_general(w_ref[...], c_ref[...], (((1,), (1,)), ((), ())),
                                 preferred_element_type=F32).astype(o_ref.dtype)


def _v_up_transposed(w_vt, ckv):
    hv, r = w_vt.shape
    nk = ckv.shape[0]
    th = _tile(hv, 2 * V_DIM, V_DIM)
    tn = _tile(nk, 13 * MXU_DEPTH, LANES)
    return pl.pallas_call(
        _vt_body, grid=(hv // th, nk // tn),
        in_specs=[pl.BlockSpec((th, r), lambda i, j: (i, 0)), pl.BlockSpec((tn, r), lambda i, j: (j, 0))],
        out_specs=pl.BlockSpec((th, tn), lambda i, j: (i, j)),
        out_shape=jax.ShapeDtypeStruct((hv, nk), BF16), name="mla_v_up_transposed",
        compiler_params=_params(("parallel", "parallel")),
    )(w_vt, ckv)


def _filter_body(z_ref, w1_ref, w2_ref, w3_ref, bf_ref, w4f_ref, w4b_ref, dl_ref, o_ref, h_ref):
    @pl.when(pl.program_id(1) == 0)
    def _():
        bf = bf_ref[...]
        h = jnp.sin(bf[3:4, :] * (_dot3(z_ref[0], w1_ref[...]) + bf[0:1, :]))
        h = jnp.sin(bf[4:5, :] * (_dot3(h, w2_ref[...]) + bf[1:2, :]))
        h_ref[...] = jnp.sin(bf[5:6, :] * (_dot3(h, w3_ref[...]) + bf[2:3, :]))

    z = z_ref[0]
    zw = z.shape[1] // 2
    sign_col = HY_BANDS * 2 + 1
    h = h_ref[...].astype(BF16)
    half = h.shape[0]
    dl = dl_ref[...]
    for part, w4_ref in enumerate((w4f_ref, w4b_ref)):
        t = z[:, part * zw:part * zw + 1]
        sign = z[:, part * zw + sign_col:part * zw + sign_col + 1]
        filt = _dot(h, w4_ref[...].astype(BF16))
        o_ref[part * half:(part + 1) * half, :] = (sign * filt * jnp.exp(-t * dl)).astype(o_ref.dtype)


def _hyena_filter(n, d, f_w1, f_b1, f_w2, f_b2, f_w3, f_b3, f_freq, f_w4):
    fh = f_w1.shape[1]
    emb = f_w1.shape[0]
    r = jnp.arange(2 * n, dtype=jnp.int32)
    p = jnp.minimum(jnp.where(r < n, r, 2 * n - r), n - 1)
    pf = p.astype(F32)[:, None]
    t = pf / (n - 1)
    w = (2.0 * math.pi / n) * pf
    bands = jnp.linspace(1e-4, HY_BANDS - 1, HY_BANDS, dtype=F32)
    sign = jnp.where(r < n, 1.0, jnp.where(r == n, 0.0, -1.0)).astype(F32)[:, None]
    zw = LANES // 2
    z = jnp.concatenate([t, jnp.cos(bands * w), -jnp.sin(bands * w), sign,
                         jnp.zeros((2 * n, zw - emb - 1), F32)], axis=-1)
    w1p = jnp.concatenate([f_w1.astype(F32), jnp.zeros((zw - emb, fh), F32)], axis=0)
    bf = jnp.concatenate([f_b1.reshape(1, fh), f_b2.reshape(1, fh), f_b3.reshape(1, fh),
                          f_freq.reshape(3, fh), jnp.zeros((2, fh), F32)], axis=0).astype(F32)
    deltas = jnp.abs(jnp.linspace(math.log(HY_DECAY_TARGET) / HY_SLOW_DECAY_PCT,
                                  math.log(HY_DECAY_TARGET) / HY_FAST_DECAY_PCT, d, dtype=F32)).reshape(1, d)
    n2 = DFT_INNER
    n1 = 2 * n // n2
    zt = z.reshape(n1, n2, zw).transpose(1, 0, 2)
    hk = n1 // 2
    zt = jnp.concatenate([zt[:, :hk], zt[:, hk:]], axis=-1)
    twice = lambda w: jnp.kron(jnp.eye(2, dtype=F32), w.astype(F32))
    bf2 = jnp.concatenate([bf, bf], axis=1)
    zero = jnp.zeros((fh, d), F32)
    w4 = f_w4.astype(F32)
    w4 = jnp.concatenate([jnp.concatenate([w4[:, :d], zero], axis=0),
                          jnp.concatenate([zero, w4[:, d:]], axis=0)], axis=1)
    tc = _tile(d, 2048, LANES)
    ncb = d // tc
    const = lambda shape: pl.BlockSpec(shape, lambda i, j: (0, 0))
    return pl.pallas_call(
        _filter_body, grid=(n2, ncb),
        in_specs=[pl.BlockSpec((1, hk, 2 * zw), lambda i, j: (i, 0, 0)), const((2 * zw, 2 * fh)),
                  const((2 * fh, 2 * fh)), const((2 * fh, 2 * fh)), const((SUBLANES, 2 * fh)),
                  pl.BlockSpec((2 * fh, tc), lambda i, j: (0, j)),
                  pl.BlockSpec((2 * fh, tc), lambda i, j: (0, j + ncb)),
                  pl.BlockSpec((1, tc), lambda i, j: (0, j))],
        out_specs=pl.BlockSpec((n1, tc), lambda i, j: (0, i * ncb + j)),
        out_shape=jax.ShapeDtypeStruct((n1, n2 * d), BF16),
        scratch_shapes=[pltpu.VMEM((hk, 2 * fh), F32)], name="hyena_filter",
        compiler_params=_params(("parallel", "arbitrary")),
    )(zt, twice(w1p), twice(f_w2), twice(f_w3), bf2, w4, w4, deltas)


def _dft_tables(n):
    big_n = 2 * n
    n2 = DFT_INNER
    n1 = big_n // n2
    hk = n1 // 2
    k1 = jnp.arange(hk, dtype=jnp.int32)
    m1 = jnp.arange(n1, dtype=jnp.int32)
    ang_a = (math.pi / n1) * ((m1[None, :] * (2 * k1[:, None] + 1)) % (2 * n1)).astype(F32)
    fwd = jnp.concatenate([jnp.cos(ang_a), -jnp.sin(ang_a)], axis=0)
    inv = (2.0 / big_n) * jnp.concatenate([jnp.cos(ang_a[:, :hk]).T, -jnp.sin(ang_a[:, :hk]).T], axis=1)
    k2 = jnp.arange(n2, dtype=jnp.int32)
    m2 = jnp.arange(n2, dtype=jnp.int32)
    freq = 2 * k1[:, None, None] + 1 + 2 * n1 * k2[None, :, None]
    ang_c = (math.pi / big_n) * ((m2[None, None, :] * freq) % (2 * big_n)).astype(F32)
    gr, gi = jnp.cos(ang_c), -jnp.sin(ang_c)
    mid = jnp.concatenate([jnp.concatenate([gr, -gi], axis=2), jnp.concatenate([gi, gr], axis=2)], axis=1)
    mid_t = jnp.swapaxes(mid, 1, 2)
    return fwd, inv, mid, mid_t


def _dft_rows_body(f_ref, x_ref, o_ref):
    o_ref[...] = _dot(f_ref[...], x_ref[...]).astype(o_ref.dtype)


def _dft_rows(f, x2d, tn):
    r, k = f.shape
    c = x2d.shape[1]
    return pl.pallas_call(
        _dft_rows_body, grid=(c // tn,),
        in_specs=[pl.BlockSpec((r, k), lambda j: (0, 0)), pl.BlockSpec((k, tn), lambda j: (0, j))],
        out_specs=pl.BlockSpec((r, tn), lambda j: (0, j)),
        out_shape=jax.ShapeDtypeStruct((r, c), BF16), name="hyena_dft_outer",
        compiler_params=_params(("parallel",)),
    )(f.astype(BF16), x2d)


def _stack_re_im(ref):
    _, _, n2, tc = ref.shape
    return ref[:, 0].reshape(2 * n2, tc)


def _dft_mid_body(m_ref, a_ref, o_ref):
    n2 = a_ref.shape[2]
    s = _dot(m_ref[0], _stack_re_im(a_ref))
    o_ref[0, 0] = s[:n2].astype(o_ref.dtype)
    o_ref[1, 0] = s[n2:].astype(o_ref.dtype)


def _dft_conv_body(m_ref, t_ref, a_ref, g_ref, o_ref):
    n2 = a_ref.shape[2]
    s = _dot(m_ref[0], _stack_re_im(a_ref))
    sr, si = s[:n2], s[n2:]
    gr, gi = g_ref[0, 0].astype(F32), g_ref[1, 0].astype(F32)
    y = jnp.concatenate([sr * gr - si * gi, sr * gi + si * gr], axis=0).astype(BF16)
    b = _dot(t_ref[0], y)
    o_ref[0, 0] = b[:n2].astype(o_ref.dtype)
    o_ref[1, 0] = b[n2:].astype(o_ref.dtype)


def _dft_mid(mid, a4, spec4=None, mid_t=None):
    _, hk, n2, d = a4.shape
    tc = _tile(d, 4096, LANES)
    mspec = pl.BlockSpec((1, 2 * n2, 2 * n2), lambda k, j: (k, 0, 0))
    dspec = pl.BlockSpec((2, 1, n2, tc), lambda k, j: (0, k, 0, j))
    if spec4 is None:
        body, in_specs, operands, name = _dft_mid_body, [mspec, dspec], [mid.astype(BF16), a4], "hyena_dft_inner"
    else:
        body, in_specs = _dft_conv_body, [mspec, mspec, dspec, dspec]
        operands, name = [mid.astype(BF16), mid_t.astype(BF16), a4, spec4], "hyena_dft_inner_conv"
    return pl.pallas_call(
        body, grid=(hk, d // tc), in_specs=in_specs, out_specs=dspec,
        out_shape=jax.ShapeDtypeStruct(a4.shape, BF16), name=name,
        compiler_params=_params(("parallel", "arbitrary")),
    )(*operands)


def _row_tile(m):
    return _tile(m, 1024, 16)


def _short_conv(h, w_in, w_conv, w_out):
    m, d = h.shape
    tm = _row_tile(m)
    tn_in = _tile(d, 256, LANES)
    ncb = d // tn_in
    b, u = _matmul(h, [(w_in, 0), (w_in, ncb), (w_in, 2 * ncb)], [(d, tn_in, BF16), (d, tn_in, BF16)],
                   _ep_gate_pair, tm=tm, tn=tn_in, name="sc_in_proj")
    tn = _tile(d, 512, LANES)
    tmo = _tile(m, 512, HALO_ROWS)
    w8 = jnp.concatenate([w_conv.astype(F32), jnp.zeros((SUBLANES - 3, d), F32)], axis=0)
    main, prev, nxt = _row_halo_specs(tmo, d, m)
    lhs = [(b, *main), (u, *main), (u, *prev), (u, *nxt), (w8, (SUBLANES, d), lambda i, j, k: (0, 0))]
    return _matmul(lhs, [(w_out, 0)], [(d, tn, BF16)], _ep_plain, tm=tmo, tn=tn, name="sc_out_proj",
                   lhs_fn=_lhs_sc_gate, lhs_shape=(m, d))[0]


def _ffn(h2, w_gate_up, w_down):
    m, d = h2.shape
    f = w_down[0].shape[1]
    tm = _row_tile(m)
    tf = _tile(f, 512, LANES)
    a = _matmul(h2, [(w_gate_up, 0), (w_gate_up, f // tf)], [(f, tf, BF16)], _ep_swiglu,
                tm=tm, tn=tf, name="ffn_gate_up")[0]
    tn = _tile(d, 512, LANES)
    return _matmul(a, [(w_down, 0)], [(d, tn, BF16)], _ep_plain, tm=_tile(m, 512, 16), tn=tn, name="ffn_down")[0]


def _rope_tables(n, n_ctx, scale):
    rows = n // GRID_W
    row = jnp.repeat(jnp.arange(rows, dtype=F32), GRID_W)
    col = jnp.tile(jnp.arange(GRID_W, dtype=F32), rows)
    axis_dim = ROPE_DIM // 2
    inv = ROPE_BASE ** (-jnp.arange(0, axis_dim, 2, dtype=F32) / axis_dim)
    ang_r, ang_c = row[:, None] * inv, col[:, None] * inv
    cos = jnp.concatenate([jnp.cos(ang_r), jnp.cos(ang_c)], axis=1)
    sin = jnp.concatenate([jnp.sin(ang_r), jnp.sin(ang_c)], axis=1)
    rot = jnp.concatenate([cos, cos, -sin, sin], axis=1)
    tab_q = scale * jnp.concatenate([jnp.ones((n, NOPE_DIM), F32), rot], axis=1)
    ctx_rot = jnp.concatenate([jnp.ones((n_ctx, ROPE_DIM), F32), jnp.zeros((n_ctx, ROPE_DIM), F32)], axis=1)
    return tab_q, rot, ctx_rot


def _mla(h, hc, w_down, q_norm, kv_norm, w_uq, w_ukv, w_out):
    n, d = h.shape
    n_ctx = hc.shape[0]
    q_rank, kv_rank = w_uq.shape[0], w_ukv.shape[0]
    heads = w_uq.shape[1] // (NOPE_DIM + ROPE_DIM)
    scale = (NOPE_DIM + ROPE_DIM) ** -0.5 * math.log2(math.e)
    tab_q, tab_k, tab_kc = _rope_tables(n, n_ctx, scale)

    def rope_cols(w):
        qd = ROPE_DIM // 4
        a = jnp.concatenate([w[..., 0:qd], w[..., 2 * qd:3 * qd]], axis=-1)
        b = jnp.concatenate([w[..., qd:2 * qd], w[..., 3 * qd:4 * qd]], axis=-1)
        return jnp.concatenate([a, b, b, a], axis=-1)

    w_dq = w_down[:, :q_rank].astype(BF16)
    w_dkv = jnp.concatenate([w_down[:, q_rank:q_rank + kv_rank], rope_cols(w_down[:, q_rank + kv_rank:])],
                            axis=1).astype(BF16)
    wq3 = w_uq.reshape(q_rank, heads, NOPE_DIM + ROPE_DIM)
    w_uq_p = jnp.concatenate([wq3[:, :, :NOPE_DIM], rope_cols(wq3[:, :, NOPE_DIM:])], axis=2)
    qw = NOPE_DIM + 2 * ROPE_DIM
    w_uq_p = w_uq_p.reshape(q_rank, heads * qw).astype(BF16)
    wkv3 = w_ukv.reshape(kv_rank, heads, NOPE_DIM + V_DIM)
    w_kn = wkv3[:, :, :NOPE_DIM].reshape(kv_rank, heads * NOPE_DIM).astype(BF16)
    w_vt = wkv3[:, :, NOPE_DIM:].reshape(kv_rank, heads * V_DIM).T.astype(BF16)

    tm = _row_tile(n)
    cqn = _matmul(h, [(w_dq, 0)], [(q_rank, q_rank, BF16)], _ep_rms,
                  extras=[(q_norm.reshape(1, q_rank).astype(F32), (1, q_rank), lambda i, j, k: (0, 0))],
                  tm=tm, tn=q_rank, name="mla_q_down")[0]

    def kv_down(hh, tab, name):
        m = hh.shape[0]
        tmk = _row_tile(m)
        wd = kv_rank + 2 * ROPE_DIM
        return _matmul(hh, [(w_dkv, 0)], [(kv_rank, kv_rank, BF16), (LANES, LANES, BF16)],
                       functools.partial(_ep_kv_down, kv_rank=kv_rank),
                       extras=[(kv_norm.reshape(1, kv_rank).astype(F32), (1, kv_rank), lambda i, j, k: (0, 0)),
                               (tab, (tmk, 2 * ROPE_DIM), lambda i, j, k: (i, 0))],
                       tm=tmk, tn=wd, name=name)

    ckv, kr = kv_down(h, tab_k, "mla_kv_down")
    ckv_c, kr_c = kv_down(hc, tab_kc, "mla_kv_down_ctx")
    ckv = jnp.concatenate([ckv, ckv_c], axis=0)
    kr = jnp.concatenate([kr, kr_c], axis=0)
    nk = n + n_ctx

    gq = 2 if heads % 2 == 0 else 1
    q = _q_up_transposed(w_uq_p.T, cqn, tab_q.T)
    tmk = _tile(nk, 1664, 16)
    k = _matmul(ckv, [(w_kn, 0)], [(heads * qw, gq * qw, BF16)], _ep_k_up,
                extras=[(kr, (tmk, LANES), lambda i, j, k: (i, 0))], tm=tmk, tn=gq * NOPE_DIM, name="mla_k_up")[0]
    vt = _v_up_transposed(w_vt, ckv)
    o = _attention(q, k, vt, heads)
    tn = _tile(d, 512, LANES)
    return _matmul(o, [(w_out, 0)], [(d, tn, BF16)], _ep_plain, tm=tm, tn=tn, name="mla_out_proj")[0]


def _hyena(h, w_in, conv_w, conv_b, f_w1, f_b1, f_w2, f_b2, f_w3, f_b3, f_freq, f_w4, skip, w_out):
    n, d = h.shape
    tm = _row_tile(n)
    tn = _tile(d, 512, LANES)
    z = _matmul(h, [(w_in, 0)], [(3 * d, tn, BF16)], _ep_plain, tm=tm, tn=tn, name="hy_in_proj")[0]
    x0, vv = _hy_gate(z, conv_w, conv_b)
    n2 = DFT_INNER
    n1 = 2 * n // n2
    hk = n1 // 2
    fwd, inv, mid, mid_t = _dft_tables(n)
    g = _hyena_filter(n, d, f_w1, f_b1, f_w2, f_b2, f_w3, f_b3, f_freq, f_w4)
    tcol = _tile(n2 * d, 4096, LANES)
    ga = _dft_rows(fwd, g, tcol)
    spec = _dft_mid(mid, ga.reshape(2, hk, n2, d))
    va = _dft_rows(fwd[:, :hk], vv.reshape(hk, n2 * d), tcol)
    vb = _dft_mid(mid, va.reshape(2, hk, n2, d), spec, mid_t)
    y = _dft_rows(inv, vb.reshape(n1, n2 * d), tcol).reshape(n, d)
    tmo = _tile(n, 512, 16)
    row = ((tmo, d), lambda i, j, k: (i, 0))
    skip_rows = jnp.broadcast_to(skip.reshape(1, d).astype(BF16), (HALO_ROWS, d))
    lhs = [(x0, *row), (y, *row), (vv, *row), (skip_rows, (HALO_ROWS, d), lambda i, j, k: (0, 0))]
    return _matmul(lhs, [(w_out, 0)], [(d, tn, BF16)], _ep_plain, tm=tmo, tn=tn, name="hy_out_proj",
                   lhs_fn=_lhs_hy_gate, lhs_shape=(n, d))[0]


def kernel(x, c, ctx, c_ctx, ada_down, ada_up, ada_bias, norm_gain, ffn_w_gate_up, ffn_w_down, sc_w_in, sc_conv, sc_w_out, mla_w_down, mla_q_norm, mla_kv_norm, mla_w_uq, mla_w_ukv, mla_w_out, hy_w_in, hy_conv, hy_conv_b, hy_f_w1, hy_f_b1, hy_f_w2, hy_f_b2, hy_f_w3, hy_f_b3, hy_f_freq, hy_f_w4, hy_skip, hy_w_out):
    batch, n, d = x.shape
    assert batch == 1 and c.shape[0] == 1 and ctx.shape[0] == 1
    depth = ada_down.shape[0]
    n_mixers = 3
    xs = x.reshape(n, d)
    cs = ctx.reshape(ctx.shape[1], d)

    mla_layers = [i for i in range(depth) if i % n_mixers == 1]
    last_ctx_read = mla_layers[-1] if mla_layers else -1

    s_raw = jnp.concatenate([c.reshape(1, d), c_ctx.reshape(1, d), jnp.zeros((2 * SUBLANES - 2, d), F32)], axis=0)
    mods = _adaln(s_raw, ada_down, ada_up, ada_bias)

    def mod_vecs(i, row):
        return [mods[i, row, m * d:(m + 1) * d] for m in range(N_MOD)]

    ffn_gu, ffn_dn = ffn_w_gate_up, ffn_w_down.astype(BF16)
    sc_in, sc_out = sc_w_in, sc_w_out.astype(BF16)
    hy_in, hy_out = hy_w_in, hy_w_out.astype(BF16)
    mla_out = mla_w_out.astype(BF16)
    pend = None
    pend_c = None
    for i in range(depth):
        kind, j = i % n_mixers, i // n_mixers
        ctx_full = i < last_ctx_read
        ctx_keys = i == last_ctx_read
        g = norm_gain[i]
        streams = [(0, xs, pend)]
        if ctx_full or ctx_keys:
            streams.append((1, cs, pend_c))
        hs = {}
        cur = {}
        for row, xv, pd in streams:
            mv = mod_vecs(i, row)
            if pd is None:
                _, hh = _resid_norm_mod(xv, None, _vec_rows(d, g[0], g[0], g[0], mv[0], mv[1]), has_h=True)
            else:
                xv, hh = _resid_norm_mod(xv, pd[0], _vec_rows(d, pd[1], pd[2], g[0], mv[0], mv[1]), has_h=True)
            hs[row], cur[row] = hh, xv

        ys = {}
        if kind == 0:
            w_in, w_out = (sc_in, j), (sc_out, j)
            ys[0] = _short_conv(hs[0], w_in, sc_conv[j], w_out)
            if ctx_full:
                ys[1] = _short_conv(hs[1], w_in, sc_conv[j], w_out)
        elif kind == 1:
            ys[0] = _mla(hs[0], hs[1], mla_w_down[j], mla_q_norm[j], mla_kv_norm[j], mla_w_uq[j], mla_w_ukv[j],
                         (mla_out, j))
            assert not ctx_full
        else:
            hp = ((hy_in, j), hy_conv[j], hy_conv_b[j], hy_f_w1[j], hy_f_b1[j], hy_f_w2[j], hy_f_b2[j],
                  hy_f_w3[j], hy_f_b3[j], hy_f_freq[j], hy_f_w4[j], hy_skip[j], (hy_out, j))
            ys[0] = _hyena(hs[0], *hp)
            if ctx_full:
                ys[1] = _hyena(hs[1], *hp)

        w_gu, w_dn = (ffn_gu, i), (ffn_dn, i)
        new_pend = {0: None, 1: None}
        for row in ys:
            mv = mod_vecs(i, row)
            xv, h2 = _resid_norm_mod(cur[row], ys[row], _vec_rows(d, mv[2], g[1], g[2], mv[3], mv[4]), has_h=True)
            cur[row] = xv
            new_pend[row] = (_ffn(h2, w_gu, w_dn), mv[5], g[3])
        xs, pend = cur[0], new_pend[0]
        if ctx_full:
            cs, pend_c = cur[1], new_pend[1]
        else:
            pend_c = None

    xs, _ = _resid_norm_mod(xs, pend[0], _vec_rows(d, pend[1], pend[2], pend[2], pend[1], pend[1]), has_h=False)
    return xs.reshape(batch, n, d)
```
